```python
import jax, jax.numpy as jnp
from jax import lax
import numpy as np

D_MODEL = 2048
BATCH = 4
SEQ = 2048
DEPTH = 1

D_MIX = D_MODEL
D_GMLP = D_MIX // 2
D_LRU = D_MIX - D_GMLP
CHUNK = 128
GMLP_HEADS = 8
GMLP_HEAD_DIM = D_GMLP // GMLP_HEADS
LRU_HEADS = 16
LRU_HEAD_DIM = D_LRU // LRU_HEADS
LRU_CONV = 4
LRU_C = 8.0
FFN_DIM = 3 * D_MODEL
FFN_CONV = 3
N_MOD = 6
EPS = 1e-6
D_IN = 2 * D_GMLP + 2 * D_LRU

kernel_name = "hybrid_gmlp_rglru_convffn_block"


def rms_norm(x, g):
    xf = x.astype(jnp.float32)
    y = xf * lax.rsqrt(jnp.mean(xf * xf, axis=-1, keepdims=True) + EPS)
    return (y * g.astype(jnp.float32)).astype(x.dtype)


def causal_dwconv(x, w, b):
    K = w.shape[0]
    S = x.shape[1]
    xp = jnp.pad(x, ((0, 0), (K - 1, 0), (0, 0)))
    y = b
    for k in range(K):
        y = y + w[k] * xp[:, k:k + S]
    return y


def gmlp_spatial_gating(u, v, ln_g, ln_b, w_s, b_s):
    B, S, _ = u.shape
    n = S // CHUNK
    vh = v.reshape(B, n, CHUNK, GMLP_HEADS, GMLP_HEAD_DIM)
    vf = vh.astype(jnp.float32)
    mu = jnp.mean(vf, axis=-1, keepdims=True)
    var = jnp.mean(jnp.square(vf - mu), axis=-1, keepdims=True)
    g = ln_g.reshape(GMLP_HEADS, GMLP_HEAD_DIM).astype(jnp.float32)
    bb = ln_b.reshape(GMLP_HEADS, GMLP_HEAD_DIM).astype(jnp.float32)
    vn = ((vf - mu) * lax.rsqrt(var + EPS) * g + bb).astype(v.dtype)
    causal = jnp.tril(jnp.ones((CHUNK, CHUNK), dtype=bool))
    ws = jnp.where(causal[None], w_s, jnp.zeros_like(w_s))
    mixed = jnp.einsum('hts,bnshd->bnthd', ws, vn) + b_s.T[None, None, :, :, None]
    uh = u.reshape(B, n, CHUNK, GMLP_HEADS, GMLP_HEAD_DIM)
    return (uh * mixed).reshape(B, S, D_GMLP)


def rg_lru(x, w_r, b_r, w_i, b_i, lam):
    B, S, _ = x.shape
    xh = x.reshape(B, S, LRU_HEADS, LRU_HEAD_DIM)
    r = jax.nn.sigmoid(jnp.einsum('bshi,hio->bsho', xh, w_r).reshape(B, S, D_LRU) + b_r)
    i = jax.nn.sigmoid(jnp.einsum('bshi,hio->bsho', xh, w_i).reshape(B, S, D_LRU) + b_i)
    log_a = (-LRU_C * r.astype(jnp.float32)) * jax.nn.softplus(-lam.astype(jnp.float32))
    a = jnp.exp(log_a)
    mult = jnp.sqrt(-jnp.expm1(2.0 * log_a))
    bterm = mult * (i.astype(jnp.float32) * x.astype(jnp.float32))

    def combine(left, right):
        a1, b1 = left
        a2, b2 = right
        return a1 * a2, a2 * b1 + b2

    _, h = lax.associative_scan(combine, (a, bterm), axis=1)
    return h.astype(x.dtype)


def setup_inputs(seed: int = 0) -> dict:
    key = jax.random.key(seed)
    ks = jax.random.split(key, 26)
    L = DEPTH

    def nrm(k, shape, s):
        return jax.random.normal(k, shape, jnp.float32) * s

    u = jax.random.uniform(ks[16], (L, D_LRU), jnp.float32, minval=0.9, maxval=0.999)
    s = u ** (1.0 / LRU_C)
    lru_lambda = jnp.log(s) - jnp.log1p(-s)
    return {
        "x": nrm(ks[0], (BATCH, SEQ, D_MODEL), 1.0),
        "c": nrm(ks[1], (BATCH, D_MODEL), 1.0),
        "w_ada": nrm(ks[2], (L, D_MODEL, N_MOD * D_MODEL), 0.5 * D_MODEL ** -0.5),
        "b_ada": nrm(ks[3], (L, N_MOD * D_MODEL), 0.02),
        "norm1": 1.0 + nrm(ks[4], (L, D_MODEL), 0.02),
        "w_in": nrm(ks[5], (L, D_MODEL, D_IN), D_MODEL ** -0.5),
        "gmlp_ln_g": 1.0 + nrm(ks[6], (L, D_GMLP), 0.02),
        "gmlp_ln_b": nrm(ks[7], (L, D_GMLP), 0.02),
        "gmlp_w_s": nrm(ks[8], (L, GMLP_HEADS, CHUNK, CHUNK), 0.5 * CHUNK ** -0.5),
        "gmlp_b_s": 1.0 + nrm(ks[9], (L, GMLP_HEADS, CHUNK), 0.1),
        "lru_conv_w": nrm(ks[10], (L, LRU_CONV, D_LRU), LRU_CONV ** -0.5),
        "lru_conv_b": nrm(ks[11], (L, D_LRU), 0.02),
        "lru_w_r": nrm(ks[12], (L, LRU_HEADS, LRU_HEAD_DIM, LRU_HEAD_DIM), LRU_HEAD_DIM ** -0.5),
        "lru_b_r": nrm(ks[13], (L, D_LRU), 0.02),
        "lru_w_i": nrm(ks[14], (L, LRU_HEADS, LRU_HEAD_DIM, LRU_HEAD_DIM), LRU_HEAD_DIM ** -0.5),
        "lru_b_i": nrm(ks[15], (L, D_LRU), 0.02),
        "lru_lambda": lru_lambda,
        "out_norm_gmlp": 1.0 + nrm(ks[17], (L, D_GMLP), 0.02),
        "out_norm_lru": 1.0 + nrm(ks[18], (L, D_LRU), 0.02),
        "w_out": nrm(ks[19], (L, D_MIX, D_MODEL), D_MIX ** -0.5),
        "norm2": 1.0 + nrm(ks[20], (L, D_MODEL), 0.02),
        "w_up": nrm(ks[21], (L, D_MODEL, 2 * FFN_DIM), D_MODEL ** -0.5),
        "ffn_conv_w": nrm(ks[22], (L, FFN_CONV, FFN_DIM), FFN_CONV ** -0.5),
        "ffn_conv_b": nrm(ks[23], (L, FFN_DIM), 0.02),
        "w_down": nrm(ks[24], (L, FFN_DIM, D_MODEL), FFN_DIM ** -0.5),
        "norm_final": 1.0 + nrm(ks[25], (D_MODEL,), 0.02),
    }


def reference(x, c, w_ada, b_ada, norm1, w_in, gmlp_ln_g, gmlp_ln_b, gmlp_w_s, gmlp_b_s,
              lru_conv_w, lru_conv_b, lru_w_r, lru_b_r, lru_w_i, lru_b_i, lru_lambda,
              out_norm_gmlp, out_norm_lru, w_out, norm2, w_up, ffn_conv_w, ffn_conv_b,
              w_down, norm_final):
    c_act = jax.nn.silu(c)
    for l in range(DEPTH):
        mod = jnp.einsum('bd,de->be', c_act, w_ada[l]) + b_ada[l]
        sh1, sc1, g1, sh2, sc2, g2 = [m[:, None, :] for m in jnp.split(mod, N_MOD, axis=-1)]

        h = rms_norm(x, norm1[l]) * (1 + sc1) + sh1
        z = jnp.einsum('bsd,de->bse', h, w_in[l])
        u, v, xb, gb = jnp.split(z, [D_GMLP, 2 * D_GMLP, 2 * D_GMLP + D_LRU], axis=-1)
        ya = gmlp_spatial_gating(jax.nn.gelu(u), jax.nn.gelu(v), gmlp_ln_g[l], gmlp_ln_b[l],
                                 gmlp_w_s[l], gmlp_b_s[l])
        xb = causal_dwconv(xb, lru_conv_w[l], lru_conv_b[l])
        yb = rg_lru(xb, lru_w_r[l], lru_b_r[l], lru_w_i[l], lru_b_i[l], lru_lambda[l]) * jax.nn.gelu(gb)
        mix = jnp.concatenate([rms_norm(ya, out_norm_gmlp[l]), rms_norm(yb, out_norm_lru[l])], axis=-1)
        x = x + g1 * jnp.einsum('bse,ed->bsd', mix, w_out[l])

        h = rms_norm(x, norm2[l]) * (1 + sc2) + sh2
        gate, val = jnp.split(jnp.einsum('bsd,df->bsf', h, w_up[l]), 2, axis=-1)
        gate = causal_dwconv(gate, ffn_conv_w[l], ffn_conv_b[l])
        x = x + g2 * jnp.einsum('bsf,fd->bsd', jax.nn.gelu(gate) * val, w_down[l])
    return rms_norm(x, norm_final)
```

```python
import math

import jax
import jax.numpy as jnp
from jax import lax
from jax.experimental import pallas as pl
from jax.experimental.pallas import tpu as pltpu

F32 = jnp.float32
BF16 = jnp.bfloat16

CHUNK = 128
GMLP_HEADS = 8
GMLP_HEAD_DIM = 128
LRU_HEAD_DIM = 64
LRU_C = 8.0
N_MOD = 6
EPS = 1e-6

LANES = 128
SUBLANES = 8
VMEM_LIMIT_BYTES = 56 * 1024 * 1024

ADA_TN = 1024
MIX_TM = 256
FFN_TM = 512
FFN_TF = 512

_GELU_C0 = math.sqrt(2.0 / math.pi)
_GELU_C1 = _GELU_C0 * 0.044715


def _gelu(x):
    hx = 0.5 * x
    t = jnp.tanh(x * (_GELU_C0 + _GELU_C1 * (x * x)))
    return hx + hx * t


def _shift_rows(x, prev, k):
    ext = jnp.concatenate([prev, x], axis=0)
    n = x.shape[0]
    return ext[SUBLANES - k:SUBLANES - k + n]


def _linear_scan(a, b):
    n = a.shape[0]
    row = lax.broadcasted_iota(jnp.int32, a.shape, 0)
    k = 1
    while k < n:
        if k < SUBLANES:
            keep = row >= k
            a_sh = jnp.where(keep, pltpu.roll(a, k, axis=0), 1.0)
            b_sh = jnp.where(keep, pltpu.roll(b, k, axis=0), 0.0)
        else:
            a_sh = jnp.concatenate([jnp.ones((k, a.shape[1]), F32), a[:n - k]], axis=0)
            b_sh = jnp.concatenate([jnp.zeros((k, a.shape[1]), F32), b[:n - k]], axis=0)
        b = b + a * b_sh
        a = a * a_sh
        k *= 2
    return a, b


def _ada_kernel(c_ref, w_ref, b_ref, o_ref):
    c = c_ref[...]
    c_act = (c * jax.nn.sigmoid(c)).astype(BF16)
    w = w_ref[...].astype(BF16)
    o_ref[...] = jnp.dot(c_act, w, preferred_element_type=F32) + b_ref[...]


def _ada_call(c, w_ada, b_ada):
    bsz, d = c.shape
    n = w_ada.shape[1]
    return pl.pallas_call(
        _ada_kernel,
        grid=(n // ADA_TN,),
        in_specs=[
            pl.BlockSpec((bsz, d), lambda j: (0, 0)),
            pl.BlockSpec((d, ADA_TN), lambda j: (0, j)),
            pl.BlockSpec((1, ADA_TN), lambda j: (0, j)),
        ],
        out_specs=pl.BlockSpec((bsz, ADA_TN), lambda j: (0, j)),
        out_shape=jax.ShapeDtypeStruct((bsz, n), F32),
        compiler_params=pltpu.CompilerParams(
            dimension_semantics=("arbitrary",), vmem_limit_bytes=VMEM_LIMIT_BYTES),
        name="adaln_mod",
    )(c, w_ada, b_ada)


def _mixer_kernel(x_ref, mod_ref, n1_ref, win_ref, lng_ref, lnb_ref, ws_ref, bs_ref,
                  cw_ref, cb_ref, wg_ref, br_ref, bi_ref, lam_ref, ong_ref, onl_ref,
                  wout_ref, n2_ref, x1_ref, h2_ref, z_ref, y_ref, halo_ref, carry_ref):
    tm, d = x_ref.shape
    dg = GMLP_HEADS * GMLP_HEAD_DIM
    dl = y_ref.shape[1] - dg
    nch = tm // CHUNK

    @pl.when(pl.program_id(1) == 0)
    def _():
        halo_ref[...] = jnp.zeros_like(halo_ref)
        carry_ref[...] = jnp.zeros_like(carry_ref)

    x = x_ref[...]
    inv = lax.rsqrt(jnp.mean(x * x, axis=-1, keepdims=True) + EPS)
    h = x * inv * (n1_ref[...] * (1.0 + mod_ref[1:2, :])) + mod_ref[0:1, :]
    z_ref[...] = jnp.dot(h.astype(BF16), win_ref[...], preferred_element_type=F32)

    r_i = lax.broadcasted_iota(jnp.int32, (CHUNK, CHUNK), 0)
    c_i = lax.broadcasted_iota(jnp.int32, (CHUNK, CHUNK), 1)
    causal = r_i >= c_i
    ssq = jnp.zeros((tm, 1), F32)
    for hd in range(GMLP_HEADS):
        lo = hd * GMLP_HEAD_DIM
        hi = lo + GMLP_HEAD_DIM
        v = _gelu(z_ref[:, dg + lo:dg + hi])
        mu = jnp.mean(v, axis=-1, keepdims=True)
        dv = v - mu
        var = jnp.mean(dv * dv, axis=-1, keepdims=True)
        vn = (dv * lax.rsqrt(var + EPS) * lng_ref[:, lo:hi] + lnb_ref[:, lo:hi]).astype(BF16)
        rhs = jnp.concatenate([vn[n * CHUNK:(n + 1) * CHUNK] for n in range(nch)], axis=1)
        w = jnp.where(causal, ws_ref[hd], 0.0).astype(BF16)
        m = jnp.dot(w, rhs, preferred_element_type=F32)
        bias = bs_ref[:, lo:hi]
        mixed = jnp.concatenate(
            [m[:, n * CHUNK:(n + 1) * CHUNK] + bias for n in range(nch)], axis=0)
        ya = _gelu(z_ref[:, lo:hi]) * mixed
        y_ref[:, lo:hi] = ya
        ssq = ssq + jnp.sum(ya * ya, axis=-1, keepdims=True)
    inv_a = lax.rsqrt(ssq * (1.0 / dg) + EPS)

    ssq = jnp.zeros((tm, 1), F32)
    for g in range(dl // LANES):
        lo = g * LANES
        hi = lo + LANES
        xb = z_ref[:, 2 * dg + lo:2 * dg + hi]
        prev = halo_ref[:, lo:hi]
        halo_ref[:, lo:hi] = xb[tm - SUBLANES:tm]
        kw = cw_ref.shape[0]
        xc = cb_ref[:, lo:hi] + cw_ref[kw - 1:kw, lo:hi] * xb
        for k in range(1, kw):
            xc = xc + cw_ref[kw - 1 - k:kw - k, lo:hi] * _shift_rows(xb, prev, k)
        gates = jnp.dot(xc.astype(BF16), wg_ref[g], preferred_element_type=F32)
        r = jax.nn.sigmoid(gates[:, :LANES] + br_ref[:, lo:hi])
        ig = jax.nn.sigmoid(gates[:, LANES:] + bi_ref[:, lo:hi])
        log_a = (-LRU_C * r) * jax.nn.softplus(-lam_ref[:, lo:hi])
        a = jnp.exp(log_a)
        bterm = jnp.sqrt(jnp.tanh(-log_a) * (1.0 + a * a)) * (ig * xc)
        a_cum, h_loc = _linear_scan(a, bterm)
        hseq = h_loc + a_cum * carry_ref[SUBLANES - 1:SUBLANES, lo:hi]
        carry_ref[:, lo:hi] = hseq[tm - SUBLANES:tm]
        yb = hseq * _gelu(z_ref[:, 2 * dg + dl + lo:2 * dg + dl + hi])
        y_ref[:, dg + lo:dg + hi] = yb
        ssq = ssq + jnp.sum(yb * yb, axis=-1, keepdims=True)
    inv_b = lax.rsqrt(ssq * (1.0 / dl) + EPS)

    mix = jnp.concatenate(
        [(y_ref[:, :dg] * inv_a * ong_ref[...]).astype(BF16),
         (y_ref[:, dg:] * inv_b * onl_ref[...]).astype(BF16)], axis=1)
    o = jnp.dot(mix, wout_ref[...], preferred_element_type=F32)
    x1 = x_ref[...] + mod_ref[2:3, :] * o
    x1_ref[...] = x1
    inv2 = lax.rsqrt(jnp.mean(x1 * x1, axis=-1, keepdims=True) + EPS)
    h2 = x1 * inv2 * (n2_ref[...] * (1.0 + mod_ref[4:5, :])) + mod_ref[3:4, :]
    h2_ref[...] = h2.astype(BF16)


def _mixer_call(x, mod, norm1, w_in, ln_g, ln_b, w_s, bs_full, conv_w, conv_b, w_gate,
                b_r, b_i, lam, on_g, on_l, w_out, norm2):
    bsz, s, d = x.shape
    tm = MIX_TM
    d_in = w_in.shape[1]
    dmix = w_out.shape[0]
    dl = lam.shape[1]

    def const(shape):
        zeros = (0,) * len(shape)
        return pl.BlockSpec(shape, lambda b, t: zeros, pipeline_mode=pl.Buffered(1))

    tok = pl.BlockSpec((None, tm, d), lambda b, t: (b, t, 0))
    return pl.pallas_call(
        _mixer_kernel,
        grid=(bsz, s // tm),
        in_specs=[
            tok,
            pl.BlockSpec((None, N_MOD, d), lambda b, t: (b, 0, 0)),
            const(norm1.shape), const(w_in.shape), const(ln_g.shape), const(ln_b.shape),
            const(w_s.shape), const(bs_full.shape), const(conv_w.shape), const(conv_b.shape),
            const(w_gate.shape), const(b_r.shape), const(b_i.shape), const(lam.shape),
            const(on_g.shape), const(on_l.shape), const(w_out.shape), const(norm2.shape),
        ],
        out_specs=[tok, tok],
        out_shape=[jax.ShapeDtypeStruct((bsz, s, d), F32),
                   jax.ShapeDtypeStruct((bsz, s, d), BF16)],
        scratch_shapes=[
            pltpu.VMEM((tm, d_in), F32),
            pltpu.VMEM((tm, dmix), F32),
            pltpu.VMEM((SUBLANES, dl), F32),
            pltpu.VMEM((SUBLANES, dl), F32),
        ],
        compiler_params=pltpu.CompilerParams(
            dimension_semantics=("arbitrary", "arbitrary"),
            vmem_limit_bytes=VMEM_LIMIT_BYTES),
        name="token_mixer",
    )(x, mod, norm1, w_in, ln_g, ln_b, w_s, bs_full, conv_w, conv_b, w_gate,
      b_r, b_i, lam, on_g, on_l, w_out, norm2)


def _ffn_kernel(h2_ref, x1_ref, mod_ref, wg_ref, wv_ref, cw_ref, cb_ref, wd_ref, nf_ref,
                o_ref, acc_ref, halo_ref):
    tm = h2_ref.shape[0]
    t = pl.program_id(1)
    j = pl.program_id(2)
    h = h2_ref[...]
    gate = jnp.dot(h, wg_ref[...], preferred_element_type=F32)
    val = jnp.dot(h, wv_ref[...], preferred_element_type=F32)
    prev = jnp.where(t == 0, 0.0, halo_ref[j])
    halo_ref[j] = gate[tm - SUBLANES:tm]
    kw = cw_ref.shape[0]
    conv = cb_ref[...] + cw_ref[kw - 1:kw, :] * gate
    for k in range(1, kw):
        conv = conv + cw_ref[kw - 1 - k:kw - k, :] * _shift_rows(gate, prev, k)
    act = (_gelu(conv) * val).astype(BF16)
    contrib = jnp.dot(act, wd_ref[...], preferred_element_type=F32)

    @pl.when(j == 0)
    def _():
        acc_ref[...] = contrib

    @pl.when(j > 0)
    def _():
        acc_ref[...] += contrib

    @pl.when(j == pl.num_programs(2) - 1)
    def _():
        xo = x1_ref[...] + mod_ref[N_MOD - 1:N_MOD, :] * acc_ref[...]
        inv = lax.rsqrt(jnp.mean(xo * xo, axis=-1, keepdims=True) + EPS)
        o_ref[...] = xo * inv * nf_ref[...]


def _ffn_call(h2, x1, mod, w_up, conv_w, conv_b, w_down, norm_final):
    bsz, s, d = x1.shape
    f = w_down.shape[0]
    tm, tf = FFN_TM, FFN_TF
    nf = f // tf
    tok = pl.BlockSpec((None, tm, d), lambda b, t, j: (b, t, 0))
    return pl.pallas_call(
        _ffn_kernel,
        grid=(bsz, s // tm, nf),
        in_specs=[
            tok, tok,
            pl.BlockSpec((None, N_MOD, d), lambda b, t, j: (b, 0, 0)),
            pl.BlockSpec((d, tf), lambda b, t, j: (0, j)),
            pl.BlockSpec((d, tf), lambda b, t, j: (0, j + nf)),
            pl.BlockSpec((conv_w.shape[0], tf), lambda b, t, j: (0, j)),
            pl.BlockSpec((1, tf), lambda b, t, j: (0, j)),
            pl.BlockSpec((tf, d), lambda b, t, j: (j, 0)),
            pl.BlockSpec((1, d), lambda b, t, j: (0, 0)),
        ],
        out_specs=tok,
        out_shape=jax.ShapeDtypeStruct((bsz, s, d), F32),
        scratch_shapes=[
            pltpu.VMEM((tm, d), F32),
            pltpu.VMEM((nf, SUBLANES, tf), F32),
        ],
        compiler_params=pltpu.CompilerParams(
            dimension_semantics=("arbitrary", "arbitrary", "arbitrary"),
            vmem_limit_bytes=VMEM_LIMIT_BYTES),
        name="conv_ffn",
    )(h2, x1, mod, w_up, w_up, conv_w, conv_b, w_down, norm_final)


def _pair_block_diag(w):
    hh, hd, _ = w.shape
    w = w.reshape(hh // 2, 2, hd, hd)
    z = jnp.zeros((hh // 2, hd, hd), w.dtype)
    top = jnp.concatenate([w[:, 0], z], axis=2)
    bot = jnp.concatenate([z, w[:, 1]], axis=2)
    return jnp.concatenate([top, bot], axis=1)


def kernel(x, c, w_ada, b_ada, norm1, w_in, gmlp_ln_g, gmlp_ln_b, gmlp_w_s, gmlp_b_s, lru_conv_w, lru_conv_b, lru_w_r, lru_b_r, lru_w_i, lru_b_i, lru_lambda, out_norm_gmlp, out_norm_lru, w_out, norm2, w_up, ffn_conv_w, ffn_conv_b, w_down, norm_final):
    depth = w_ada.shape[0]
    assert depth == 1, "the final RMSNorm is fused into the FFN kernel of a single layer"
    bsz, s, d = x.shape
    for l in range(depth):
        mod = _ada_call(c, w_ada[l], b_ada[l][None, :]).reshape(bsz, N_MOD, d)
        bs_full = jnp.repeat(gmlp_b_s[l].T, GMLP_HEAD_DIM, axis=1)
        w_gate = jnp.concatenate(
            [_pair_block_diag(lru_w_r[l]), _pair_block_diag(lru_w_i[l])], axis=2).astype(BF16)
        x1, h2 = _mixer_call(
            x, mod, norm1[l][None, :], w_in[l].astype(BF16),
            gmlp_ln_g[l][None, :], gmlp_ln_b[l][None, :], gmlp_w_s[l], bs_full,
            lru_conv_w[l], lru_conv_b[l][None, :], w_gate,
            lru_b_r[l][None, :], lru_b_i[l][None, :], lru_lambda[l][None, :],
            out_norm_gmlp[l][None, :], out_norm_lru[l][None, :],
            w_out[l].astype(BF16), norm2[l][None, :])
        x = _ffn_call(h2, x1, mod, w_up[l].astype(BF16), ffn_conv_w[l],
                      ffn_conv_b[l][None, :], w_down[l].astype(BF16), norm_final[None, :])
    return x
```

```python
import math

import jax
import jax.numpy as jnp
from jax import lax
from jax.experimental import pallas as pl
from jax.experimental.pallas import tpu as pltpu

F32 = jnp.float32
BF16 = jnp.bfloat16

CHUNK = 128
GMLP_HEADS = 8
GMLP_HEAD_DIM = 128
LRU_HEAD_DIM = 64
LRU_C = 8.0
N_MOD = 6
EPS = 1e-6

LANES = 128
SUBLANES = 8
VMEM_LIMIT_BYTES = 56 * 1024 * 1024

ADA_TN = 1024
MIX_TM = 256
FFN_TM = 512
FFN_TF = 512
FFN_SUB = 256

_GELU_C0 = math.sqrt(2.0 / math.pi)
_GELU_C1 = _GELU_C0 * 0.044715


def _gelu(x):
    hx = 0.5 * x
    t = jnp.tanh(x * (_GELU_C0 + _GELU_C1 * (x * x)))
    return hx + hx * t


def _shift_rows(x, prev, k):
    ext = jnp.concatenate([prev, x], axis=0)
    n = x.shape[0]
    return ext[SUBLANES - k:SUBLANES - k + n]


def _linear_scan(a, b):
    n = a.shape[0]
    row = lax.broadcasted_iota(jnp.int32, a.shape, 0)
    k = 1
    while k < n:
        if k < SUBLANES:
            keep = row >= k
            a_sh = jnp.where(keep, pltpu.roll(a, k, axis=0), 1.0)
            b_sh = jnp.where(keep, pltpu.roll(b, k, axis=0), 0.0)
        else:
            a_sh = jnp.concatenate([jnp.ones((k, a.shape[1]), F32), a[:n - k]], axis=0)
            b_sh = jnp.concatenate([jnp.zeros((k, a.shape[1]), F32), b[:n - k]], axis=0)
        b = b + a * b_sh
        a = a * a_sh
        k *= 2
    return a, b


def _ada_kernel(c_ref, w_ref, b_ref, o_ref):
    c = c_ref[...]
    c_act = (c * jax.nn.sigmoid(c)).astype(BF16)
    w = w_ref[...].astype(BF16)
    o_ref[...] = jnp.dot(c_act, w, preferred_element_type=F32) + b_ref[...]


def _ada_call(c, w_ada, b_ada):
    bsz, d = c.shape
    n = w_ada.shape[1]
    return pl.pallas_call(
        _ada_kernel,
        grid=(n // ADA_TN,),
        in_specs=[
            pl.BlockSpec((bsz, d), lambda j: (0, 0)),
            pl.BlockSpec((d, ADA_TN), lambda j: (0, j)),
            pl.BlockSpec((1, ADA_TN), lambda j: (0, j)),
        ],
        out_specs=pl.BlockSpec((bsz, ADA_TN), lambda j: (0, j)),
        out_shape=jax.ShapeDtypeStruct((bsz, n), F32),
        compiler_params=pltpu.CompilerParams(
            dimension_semantics=("arbitrary",), vmem_limit_bytes=VMEM_LIMIT_BYTES),
        name="adaln_mod",
    )(c, w_ada, b_ada)


def _mixer_kernel(x_ref, mod_ref, n1_ref, win_ref, lng_ref, lnb_ref, ws_ref, bs_ref,
                  cw_ref, cb_ref, wg_ref, br_ref, bi_ref, lam_ref, ong_ref, onl_ref,
                  wout_ref, n2_ref, x1_ref, h2_ref, z_ref, y_ref, halo_ref, carry_ref):
    tm, d = x_ref.shape
    dg = GMLP_HEADS * GMLP_HEAD_DIM
    dl = y_ref.shape[1] - dg
    nch = tm // CHUNK

    @pl.when(pl.program_id(1) == 0)
    def _():
        halo_ref[...] = jnp.zeros_like(halo_ref)
        carry_ref[...] = jnp.zeros_like(carry_ref)

    x = x_ref[...]
    inv = lax.rsqrt(jnp.mean(x * x, axis=-1, keepdims=True) + EPS)
    h = x * inv * (n1_ref[...] * (1.0 + mod_ref[1:2, :])) + mod_ref[0:1, :]
    z_ref[...] = jnp.dot(h.astype(BF16), win_ref[...], preferred_element_type=F32)

    r_i = lax.broadcasted_iota(jnp.int32, (CHUNK, CHUNK), 0)
    c_i = lax.broadcasted_iota(jnp.int32, (CHUNK, CHUNK), 1)
    causal = r_i >= c_i
    ssq = jnp.zeros((tm, 1), F32)
    for hd in range(GMLP_HEADS):
        lo = hd * GMLP_HEAD_DIM
        hi = lo + GMLP_HEAD_DIM
        v = _gelu(z_ref[:, dg + lo:dg + hi])
        mu = jnp.mean(v, axis=-1, keepdims=True)
        dv = v - mu
        var = jnp.mean(dv * dv, axis=-1, keepdims=True)
        vn = (dv * lax.rsqrt(var + EPS) * lng_ref[:, lo:hi] + lnb_ref[:, lo:hi]).astype(BF16)
        rhs = jnp.concatenate([vn[n * CHUNK:(n + 1) * CHUNK] for n in range(nch)], axis=1)
        w = jnp.where(causal, ws_ref[hd], 0.0).astype(BF16)
        m = jnp.dot(w, rhs, preferred_element_type=F32)
        bias = bs_ref[:, lo:hi]
        mixed = jnp.concatenate(
            [m[:, n * CHUNK:(n + 1) * CHUNK] + bias for n in range(nch)], axis=0)
        ya = _gelu(z_ref[:, lo:hi]) * mixed
        y_ref[:, lo:hi] = ya
        ssq = ssq + jnp.sum(ya * ya, axis=-1, keepdims=True)
    inv_a = lax.rsqrt(ssq * (1.0 / dg) + EPS)

    ssq = jnp.zeros((tm, 1), F32)
    for g in range(dl // LANES):
        lo = g * LANES
        hi = lo + LANES
        xb = z_ref[:, 2 * dg + lo:2 * dg + hi]
        prev = halo_ref[:, lo:hi]
        halo_ref[:, lo:hi] = xb[tm - SUBLANES:tm]
        kw = cw_ref.shape[0]
        xc = cb_ref[:, lo:hi] + cw_ref[kw - 1:kw, lo:hi] * xb
        for k in range(1, kw):
            xc = xc + cw_ref[kw - 1 - k:kw - k, lo:hi] * _shift_rows(xb, prev, k)
        gates = jnp.dot(xc.astype(BF16), wg_ref[g], preferred_element_type=F32)
        r = jax.nn.sigmoid(gates[:, :LANES] + br_ref[:, lo:hi])
        ig = jax.nn.sigmoid(gates[:, LANES:] + bi_ref[:, lo:hi])
        log_a = (-LRU_C * r) * jax.nn.softplus(-lam_ref[:, lo:hi])
        a = jnp.exp(log_a)
        bterm = jnp.sqrt(jnp.tanh(-log_a) * (1.0 + a * a)) * (ig * xc)
        a_cum, h_loc = _linear_scan(a, bterm)
        hseq = h_loc + a_cum * carry_ref[SUBLANES - 1:SUBLANES, lo:hi]
        carry_ref[:, lo:hi] = hseq[tm - SUBLANES:tm]
        yb = hseq * _gelu(z_ref[:, 2 * dg + dl + lo:2 * dg + dl + hi])
        y_ref[:, dg + lo:dg + hi] = yb
        ssq = ssq + jnp.sum(yb * yb, axis=-1, keepdims=True)
    inv_b = lax.rsqrt(ssq * (1.0 / dl) + EPS)

    mix = jnp.concatenate(
        [(y_ref[:, :dg] * inv_a * ong_ref[...]).astype(BF16),
         (y_ref[:, dg:] * inv_b * onl_ref[...]).astype(BF16)], axis=1)
    o = jnp.dot(mix, wout_ref[...], preferred_element_type=F32)
    x1 = x_ref[...] + mod_ref[2:3, :] * o
    x1_ref[...] = x1
    inv2 = lax.rsqrt(jnp.mean(x1 * x1, axis=-1, keepdims=True) + EPS)
    h2 = x1 * inv2 * (n2_ref[...] * (1.0 + mod_ref[4:5, :])) + mod_ref[3:4, :]
    h2_ref[...] = h2.astype(BF16)


def _mixer_call(x, mod, norm1, w_in, ln_g, ln_b, w_s, bs_full, conv_w, conv_b, w_gate,
                b_r, b_i, lam, on_g, on_l, w_out, norm2):
    bsz, s, d = x.shape
    tm = MIX_TM
    d_in = w_in.shape[1]
    dmix = w_out.shape[0]
    dl = lam.shape[1]

    def const(shape):
        zeros = (0,) * len(shape)
        return pl.BlockSpec(shape, lambda b, t: zeros, pipeline_mode=pl.Buffered(1))

    tok = pl.BlockSpec((None, tm, d), lambda b, t: (b, t, 0))
    return pl.pallas_call(
        _mixer_kernel,
        grid=(bsz, s // tm),
        in_specs=[
            tok,
            pl.BlockSpec((None, N_MOD, d), lambda b, t: (b, 0, 0)),
            const(norm1.shape), const(w_in.shape), const(ln_g.shape), const(ln_b.shape),
            const(w_s.shape), const(bs_full.shape), const(conv_w.shape), const(conv_b.shape),
            const(w_gate.shape), const(b_r.shape), const(b_i.shape), const(lam.shape),
            const(on_g.shape), const(on_l.shape), const(w_out.shape), const(norm2.shape),
        ],
        out_specs=[tok, tok],
        out_shape=[jax.ShapeDtypeStruct((bsz, s, d), F32),
                   jax.ShapeDtypeStruct((bsz, s, d), BF16)],
        scratch_shapes=[
            pltpu.VMEM((tm, d_in), F32),
            pltpu.VMEM((tm, dmix), F32),
            pltpu.VMEM((SUBLANES, dl), F32),
            pltpu.VMEM((SUBLANES, dl), F32),
        ],
        compiler_params=pltpu.CompilerParams(
            dimension_semantics=("arbitrary", "arbitrary"),
            vmem_limit_bytes=VMEM_LIMIT_BYTES),
        name="token_mixer",
    )(x, mod, norm1, w_in, ln_g, ln_b, w_s, bs_full, conv_w, conv_b, w_gate,
      b_r, b_i, lam, on_g, on_l, w_out, norm2)


def _ffn_kernel(h2_ref, x1_ref, mod_ref, wg_ref, wv_ref, cw_ref, cb_ref, wd_ref, nf_ref,
                o_ref, acc_ref, halo_ref):
    tm = h2_ref.shape[0]
    t = pl.program_id(1)
    j = pl.program_id(2)
    h = h2_ref[...]
    kw = cw_ref.shape[0]
    acts = []
    for c in range(wg_ref.shape[1] // FFN_SUB):
        lo = c * FFN_SUB
        hi = lo + FFN_SUB
        gate = jnp.dot(h, wg_ref[:, lo:hi], preferred_element_type=F32)
        val = jnp.dot(h, wv_ref[:, lo:hi], preferred_element_type=F32)
        prev = jnp.where(t == 0, 0.0, halo_ref[j, :, lo:hi])
        halo_ref[j, :, lo:hi] = gate[tm - SUBLANES:tm]
        conv = cb_ref[:, lo:hi] + cw_ref[kw - 1:kw, lo:hi] * gate
        for k in range(1, kw):
            conv = conv + cw_ref[kw - 1 - k:kw - k, lo:hi] * _shift_rows(gate, prev, k)
        acts.append((_gelu(conv) * val).astype(BF16))
    contrib = jnp.dot(jnp.concatenate(acts, axis=1), wd_ref[...], preferred_element_type=F32)
    acc_ref[...] = jnp.where(j == 0, contrib, acc_ref[...] + contrib)

    @pl.when(j == pl.num_programs(2) - 1)
    def _():
        xo = x1_ref[...] + mod_ref[N_MOD - 1:N_MOD, :] * acc_ref[...]
        inv = lax.rsqrt(jnp.mean(xo * xo, axis=-1, keepdims=True) + EPS)
        o_ref[...] = xo * inv * nf_ref[...]


def _ffn_call(h2, x1, mod, w_up, conv_w, conv_b, w_down, norm_final):
    bsz, s, d = x1.shape
    f = w_down.shape[0]
    tm, tf = FFN_TM, FFN_TF
    nf = f // tf
    tok = pl.BlockSpec((None, tm, d), lambda b, t, j: (b, t, 0))
    return pl.pallas_call(
        _ffn_kernel,
        grid=(bsz, s // tm, nf),
        in_specs=[
            tok, tok,
            pl.BlockSpec((None, N_MOD, d), lambda b, t, j: (b, 0, 0)),
            pl.BlockSpec((d, tf), lambda b, t, j: (0, j)),
            pl.BlockSpec((d, tf), lambda b, t, j: (0, j + nf)),
            pl.BlockSpec((conv_w.shape[0], tf), lambda b, t, j: (0, j)),
            pl.BlockSpec((1, tf), lambda b, t, j: (0, j)),
            pl.BlockSpec((tf, d), lambda b, t, j: (j, 0)),
            pl.BlockSpec((1, d), lambda b, t, j: (0, 0)),
        ],
        out_specs=tok,
        out_shape=jax.ShapeDtypeStruct((bsz, s, d), F32),
        scratch_shapes=[
            pltpu.VMEM((tm, d), F32),
            pltpu.VMEM((nf, SUBLANES, tf), F32),
        ],
        compiler_params=pltpu.CompilerParams(
            dimension_semantics=("arbitrary", "arbitrary", "arbitrary"),
            vmem_limit_bytes=VMEM_LIMIT_BYTES),
        name="conv_ffn",
    )(h2, x1, mod, w_up, w_up, conv_w, conv_b, w_down, norm_final)


def _pair_block_diag(w):
    hh, hd, _ = w.shape
    w = w.reshape(hh // 2, 2, hd, hd)
    z = jnp.zeros((hh // 2, hd, hd), w.dtype)
    top = jnp.concatenate([w[:, 0], z], axis=2)
    bot = jnp.concatenate([z, w[:, 1]], axis=2)
    return jnp.concatenate([top, bot], axis=1)


def kernel(x, c, w_ada, b_ada, norm1, w_in, gmlp_ln_g, gmlp_ln_b, gmlp_w_s, gmlp_b_s, lru_conv_w, lru_conv_b, lru_w_r, lru_b_r, lru_w_i, lru_b_i, lru_lambda, out_norm_gmlp, out_norm_lru, w_out, norm2, w_up, ffn_conv_w, ffn_conv_b, w_down, norm_final):
    depth = w_ada.shape[0]
    assert depth == 1, "the final RMSNorm is fused into the FFN kernel of a single layer"
    bsz, s, d = x.shape
    for l in range(depth):
        mod = _ada_call(c, w_ada[l], b_ada[l][None, :]).reshape(bsz, N_MOD, d)
        bs_full = jnp.repeat(gmlp_b_s[l].T, GMLP_HEAD_DIM, axis=1)
        w_gate = jnp.concatenate(
            [_pair_block_diag(lru_w_r[l]), _pair_block_diag(lru_w_i[l])], axis=2).astype(BF16)
        x1, h2 = _mixer_call(
            x, mod, norm1[l][None, :], w_in[l].astype(BF16),
            gmlp_ln_g[l][None, :], gmlp_ln_b[l][None, :], gmlp_w_s[l], bs_full,
            lru_conv_w[l], lru_conv_b[l][None, :], w_gate,
            lru_b_r[l][None, :], lru_b_i[l][None, :], lru_lambda[l][None, :],
            out_norm_gmlp[l][None, :], out_norm_lru[l][None, :],
            w_out[l].astype(BF16), norm2[l][None, :])
        x = _ffn_call(h2, x1, mod, w_up[l].astype(BF16), ffn_conv_w[l],
                      ffn_conv_b[l][None, :], w_down[l].astype(BF16), norm_final[None, :])
    return x
```

```python
import math

import jax
import jax.numpy as jnp
from jax import lax
from jax.experimental import pallas as pl
from jax.experimental.pallas import tpu as pltpu

F32 = jnp.float32
BF16 = jnp.bfloat16

CHUNK = 128
GMLP_HEADS = 8
GMLP_HEAD_DIM = 128
LRU_HEAD_DIM = 64
LRU_C = 8.0
N_MOD = 6
EPS = 1e-6

LANES = 128
SUBLANES = 8
VMEM_LIMIT_BYTES = 56 * 1024 * 1024

ADA_TN = 1024
MIX_TM = 256
FFN_TM = 512
FFN_TF = 512
FFN_SUB = 512

_GELU_C0 = math.sqrt(2.0 / math.pi)
_GELU_C1 = _GELU_C0 * 0.044715


def _gelu(x):
    hx = 0.5 * x
    t = jnp.tanh(x * (_GELU_C0 + _GELU_C1 * (x * x)))
    return hx + hx * t


def _shift_rows(x, prev, k):
    ext = jnp.concatenate([prev, x], axis=0)
    n = x.shape[0]
    return ext[SUBLANES - k:SUBLANES - k + n]


def _linear_scan(a, b):
    n = a.shape[0]
    row = lax.broadcasted_iota(jnp.int32, a.shape, 0)
    k = 1
    while k < n:
        if k < SUBLANES:
            keep = row >= k
            a_sh = jnp.where(keep, pltpu.roll(a, k, axis=0), 1.0)
            b_sh = jnp.where(keep, pltpu.roll(b, k, axis=0), 0.0)
        else:
            a_sh = jnp.concatenate([jnp.ones((k, a.shape[1]), F32), a[:n - k]], axis=0)
            b_sh = jnp.concatenate([jnp.zeros((k, a.shape[1]), F32), b[:n - k]], axis=0)
        b = b + a * b_sh
        a = a * a_sh
        k *= 2
    return a, b


def _ada_kernel(c_ref, w_ref, b_ref, o_ref):
    c = c_ref[...]
    c_act = (c * jax.nn.sigmoid(c)).astype(BF16)
    w = w_ref[...].astype(BF16)
    o_ref[...] = jnp.dot(c_act, w, preferred_element_type=F32) + b_ref[...]


def _ada_call(c, w_ada, b_ada):
    bsz, d = c.shape
    n = w_ada.shape[1]
    return pl.pallas_call(
        _ada_kernel,
        grid=(n // ADA_TN,),
        in_specs=[
            pl.BlockSpec((bsz, d), lambda j: (0, 0)),
            pl.BlockSpec((d, ADA_TN), lambda j: (0, j)),
            pl.BlockSpec((1, ADA_TN), lambda j: (0, j)),
        ],
        out_specs=pl.BlockSpec((bsz, ADA_TN), lambda j: (0, j)),
        out_shape=jax.ShapeDtypeStruct((bsz, n), F32),
        compiler_params=pltpu.CompilerParams(
            dimension_semantics=("arbitrary",), vmem_limit_bytes=VMEM_LIMIT_BYTES),
        name="adaln_mod",
    )(c, w_ada, b_ada)


def _mixer_kernel(x_ref, mod_ref, n1_ref, win_ref, lng_ref, lnb_ref, ws_ref, bs_ref,
                  cw_ref, cb_ref, wg_ref, br_ref, bi_ref, lam_ref, ong_ref, onl_ref,
                  wout_ref, n2_ref, x1_ref, h2_ref, z_ref, y_ref, halo_ref, carry_ref):
    tm, d = x_ref.shape
    dg = GMLP_HEADS * GMLP_HEAD_DIM
    dl = y_ref.shape[1] - dg
    nch = tm // CHUNK

    @pl.when(pl.program_id(1) == 0)
    def _():
        halo_ref[...] = jnp.zeros_like(halo_ref)
        carry_ref[...] = jnp.zeros_like(carry_ref)

    x = x_ref[...]
    inv = lax.rsqrt(jnp.mean(x * x, axis=-1, keepdims=True) + EPS)
    h = x * inv * (n1_ref[...] * (1.0 + mod_ref[1:2, :])) + mod_ref[0:1, :]
    z_ref[...] = jnp.dot(h.astype(BF16), win_ref[...], preferred_element_type=F32)

    r_i = lax.broadcasted_iota(jnp.int32, (CHUNK, CHUNK), 0)
    c_i = lax.broadcasted_iota(jnp.int32, (CHUNK, CHUNK), 1)
    causal = r_i >= c_i
    ssq = jnp.zeros((tm, 1), F32)
    for hd in range(GMLP_HEADS):
        lo = hd * GMLP_HEAD_DIM
        hi = lo + GMLP_HEAD_DIM
        v = _gelu(z_ref[:, dg + lo:dg + hi])
        mu = jnp.mean(v, axis=-1, keepdims=True)
        dv = v - mu
        var = jnp.mean(dv * dv, axis=-1, keepdims=True)
        vn = (dv * lax.rsqrt(var + EPS) * lng_ref[:, lo:hi] + lnb_ref[:, lo:hi]).astype(BF16)
        rhs = jnp.concatenate([vn[n * CHUNK:(n + 1) * CHUNK] for n in range(nch)], axis=1)
        w = jnp.where(causal, ws_ref[hd], 0.0).astype(BF16)
        m = jnp.dot(w, rhs, preferred_element_type=F32)
        bias = bs_ref[:, lo:hi]
        mixed = jnp.concatenate(
            [m[:, n * CHUNK:(n + 1) * CHUNK] + bias for n in range(nch)], axis=0)
        ya = _gelu(z_ref[:, lo:hi]) * mixed
        y_ref[:, lo:hi] = ya
        ssq = ssq + jnp.sum(ya * ya, axis=-1, keepdims=True)
    inv_a = lax.rsqrt(ssq * (1.0 / dg) + EPS)

    ssq = jnp.zeros((tm, 1), F32)
    for g in range(dl // LANES):
        lo = g * LANES
        hi = lo + LANES
        xb = z_ref[:, 2 * dg + lo:2 * dg + hi]
        prev = halo_ref[:, lo:hi]
        halo_ref[:, lo:hi] = xb[tm - SUBLANES:tm]
        kw = cw_ref.shape[0]
        xc = cb_ref[:, lo:hi] + cw_ref[kw - 1:kw, lo:hi] * xb
        for k in range(1, kw):
            xc = xc + cw_ref[kw - 1 - k:kw - k, lo:hi] * _shift_rows(xb, prev, k)
        gates = jnp.dot(xc.astype(BF16), wg_ref[g], preferred_element_type=F32)
        r = jax.nn.sigmoid(gates[:, :LANES] + br_ref[:, lo:hi])
        ig = jax.nn.sigmoid(gates[:, LANES:] + bi_ref[:, lo:hi])
        log_a = (-LRU_C * r) * jax.nn.softplus(-lam_ref[:, lo:hi])
        a = jnp.exp(log_a)
        bterm = jnp.sqrt(jnp.tanh(-log_a) * (1.0 + a * a)) * (ig * xc)
        a_cum, h_loc = _linear_scan(a, bterm)
        hseq = h_loc + a_cum * carry_ref[SUBLANES - 1:SUBLANES, lo:hi]
        carry_ref[:, lo:hi] = hseq[tm - SUBLANES:tm]
        yb = hseq * _gelu(z_ref[:, 2 * dg + dl + lo:2 * dg + dl + hi])
        y_ref[:, dg + lo:dg + hi] = yb
        ssq = ssq + jnp.sum(yb * yb, axis=-1, keepdims=True)
    inv_b = lax.rsqrt(ssq * (1.0 / dl) + EPS)

    mix = jnp.concatenate(
        [(y_ref[:, :dg] * inv_a * ong_ref[...]).astype(BF16),
         (y_ref[:, dg:] * inv_b * onl_ref[...]).astype(BF16)], axis=1)
    o = jnp.dot(mix, wout_ref[...], preferred_element_type=F32)
    x1 = x_ref[...] + mod_ref[2:3, :] * o
    x1_ref[...] = x1
    inv2 = lax.rsqrt(jnp.mean(x1 * x1, axis=-1, keepdims=True) + EPS)
    h2 = x1 * inv2 * (n2_ref[...] * (1.0 + mod_ref[4:5, :])) + mod_ref[3:4, :]
    h2_ref[...] = h2.astype(BF16)


def _mixer_call(x, mod, norm1, w_in, ln_g, ln_b, w_s, bs_full, conv_w, conv_b, w_gate,
                b_r, b_i, lam, on_g, on_l, w_out, norm2):
    bsz, s, d = x.shape
    tm = MIX_TM
    d_in = w_in.shape[1]
    dmix = w_out.shape[0]
    dl = lam.shape[1]

    def const(shape):
        zeros = (0,) * len(shape)
        return pl.BlockSpec(shape, lambda b, t: zeros, pipeline_mode=pl.Buffered(1))

    tok = pl.BlockSpec((None, tm, d), lambda b, t: (b, t, 0))
    return pl.pallas_call(
        _mixer_kernel,
        grid=(bsz, s // tm),
        in_specs=[
            tok,
            pl.BlockSpec((None, N_MOD, d), lambda b, t: (b, 0, 0)),
            const(norm1.shape), const(w_in.shape), const(ln_g.shape), const(ln_b.shape),
            const(w_s.shape), const(bs_full.shape), const(conv_w.shape), const(conv_b.shape),
            const(w_gate.shape), const(b_r.shape), const(b_i.shape), const(lam.shape),
            const(on_g.shape), const(on_l.shape), const(w_out.shape), const(norm2.shape),
        ],
        out_specs=[tok, tok],
        out_shape=[jax.ShapeDtypeStruct((bsz, s, d), F32),
                   jax.ShapeDtypeStruct((bsz, s, d), BF16)],
        scratch_shapes=[
            pltpu.VMEM((tm, d_in), F32),
            pltpu.VMEM((tm, dmix), F32),
            pltpu.VMEM((SUBLANES, dl), F32),
            pltpu.VMEM((SUBLANES, dl), F32),
        ],
        compiler_params=pltpu.CompilerParams(
            dimension_semantics=("arbitrary", "arbitrary"),
            vmem_limit_bytes=VMEM_LIMIT_BYTES),
        name="token_mixer",
    )(x, mod, norm1, w_in, ln_g, ln_b, w_s, bs_full, conv_w, conv_b, w_gate,
      b_r, b_i, lam, on_g, on_l, w_out, norm2)


def _ffn_kernel(h2_ref, x1_ref, mod_ref, wg_ref, wv_ref, cw_ref, cb_ref, wd_ref, nf_ref,
                o_ref, acc_ref, halo_ref):
    tm = h2_ref.shape[0]
    t = pl.program_id(1)
    j = pl.program_id(2)
    h = h2_ref[...]
    kw = cw_ref.shape[0]
    acts = []
    for c in range(wg_ref.shape[1] // FFN_SUB):
        lo = c * FFN_SUB
        hi = lo + FFN_SUB
        gate = jnp.dot(h, wg_ref[:, lo:hi], preferred_element_type=F32)
        val = jnp.dot(h, wv_ref[:, lo:hi], preferred_element_type=F32)
        prev = jnp.where(t == 0, 0.0, halo_ref[j, :, lo:hi])
        halo_ref[j, :, lo:hi] = gate[tm - SUBLANES:tm]
        conv = cb_ref[:, lo:hi] + cw_ref[kw - 1:kw, lo:hi] * gate
        for k in range(1, kw):
            conv = conv + cw_ref[kw - 1 - k:kw - k, lo:hi] * _shift_rows(gate, prev, k)
        acts.append((_gelu(conv) * val).astype(BF16))
    contrib = jnp.dot(jnp.concatenate(acts, axis=1), wd_ref[...], preferred_element_type=F32)
    acc_ref[...] = jnp.where(j == 0, contrib, acc_ref[...] + contrib)

    @pl.when(j == pl.num_programs(2) - 1)
    def _():
        xo = x1_ref[...] + mod_ref[N_MOD - 1:N_MOD, :] * acc_ref[...]
        inv = lax.rsqrt(jnp.mean(xo * xo, axis=-1, keepdims=True) + EPS)
        o_ref[...] = xo * inv * nf_ref[...]


def _ffn_call(h2, x1, mod, w_up, conv_w, conv_b, w_down, norm_final):
    bsz, s, d = x1.shape
    f = w_down.shape[0]
    tm, tf = FFN_TM, FFN_TF
    nf = f // tf
    tok = pl.BlockSpec((None, tm, d), lambda b, t, j: (b, t, 0))
    return pl.pallas_call(
        _ffn_kernel,
        grid=(bsz, s // tm, nf),
        in_specs=[
            tok, tok,
            pl.BlockSpec((None, N_MOD, d), lambda b, t, j: (b, 0, 0)),
            pl.BlockSpec((d, tf), lambda b, t, j: (0, j)),
            pl.BlockSpec((d, tf), lambda b, t, j: (0, j + nf)),
            pl.BlockSpec((conv_w.shape[0], tf), lambda b, t, j: (0, j)),
            pl.BlockSpec((1, tf), lambda b, t, j: (0, j)),
            pl.BlockSpec((tf, d), lambda b, t, j: (j, 0)),
            pl.BlockSpec((1, d), lambda b, t, j: (0, 0)),
        ],
        out_specs=tok,
        out_shape=jax.ShapeDtypeStruct((bsz, s, d), F32),
        scratch_shapes=[
            pltpu.VMEM((tm, d), F32),
            pltpu.VMEM((nf, SUBLANES, tf), F32),
        ],
        compiler_params=pltpu.CompilerParams(
            dimension_semantics=("arbitrary", "arbitrary", "arbitrary"),
            vmem_limit_bytes=VMEM_LIMIT_BYTES),
        name="conv_ffn",
    )(h2, x1, mod, w_up, w_up, conv_w, conv_b, w_down, norm_final)


def _pair_block_diag(w):
    hh, hd, _ = w.shape
    w = w.reshape(hh // 2, 2, hd, hd)
    z = jnp.zeros((hh // 2, hd, hd), w.dtype)
    top = jnp.concatenate([w[:, 0], z], axis=2)
    bot = jnp.concatenate([z, w[:, 1]], axis=2)
    return jnp.concatenate([top, bot], axis=1)


def kernel(x, c, w_ada, b_ada, norm1, w_in, gmlp_ln_g, gmlp_ln_b, gmlp_w_s, gmlp_b_s, lru_conv_w, lru_conv_b, lru_w_r, lru_b_r, lru_w_i, lru_b_i, lru_lambda, out_norm_gmlp, out_norm_lru, w_out, norm2, w_up, ffn_conv_w, ffn_conv_b, w_down, norm_final):
    depth = w_ada.shape[0]
    assert depth == 1, "the final RMSNorm is fused into the FFN kernel of a single layer"
    bsz, s, d = x.shape
    for l in range(depth):
        mod = _ada_call(c, w_ada[l], b_ada[l][None, :]).reshape(bsz, N_MOD, d)
        bs_full = jnp.repeat(gmlp_b_s[l].T, GMLP_HEAD_DIM, axis=1)
        w_gate = jnp.concatenate(
            [_pair_block_diag(lru_w_r[l]), _pair_block_diag(lru_w_i[l])], axis=2).astype(BF16)
        x1, h2 = _mixer_call(
            x, mod, norm1[l][None, :], w_in[l].astype(BF16),
            gmlp_ln_g[l][None, :], gmlp_ln_b[l][None, :], gmlp_w_s[l], bs_full,
            lru_conv_w[l], lru_conv_b[l][None, :], w_gate,
            lru_b_r[l][None, :], lru_b_i[l][None, :], lru_lambda[l][None, :],
            out_norm_gmlp[l][None, :], out_norm_lru[l][None, :],
            w_out[l].astype(BF16), norm2[l][None, :])
        x = _ffn_call(h2, x1, mod, w_up[l].astype(BF16), ffn_conv_w[l],
                      ffn_conv_b[l][None, :], w_down[l].astype(BF16), norm_final[None, :])
    return x
```

```python
import functools
import math

import jax
import jax.numpy as jnp
from jax import lax
from jax.experimental import pallas as pl
from jax.experimental.pallas import tpu as pltpu

F32 = jnp.float32
BF16 = jnp.bfloat16

CHUNK = 128
GMLP_HEADS = 8
GMLP_HEAD_DIM = 128
LRU_HEAD_DIM = 64
LRU_C = 8.0
N_MOD = 6
EPS = 1e-6

LANES = 128
SUBLANES = 8
VMEM_LIMIT_BYTES = 56 * 1024 * 1024

ADA_TN = 1024
MIX_TM = 256
MIX_WCOLS = 512
FFN_TM = 512
FFN_TF = 512

MIX_COST_PRENORM = 1000
MIX_COST_HEAD = 250
MIX_COST_LANE_GROUP = 625
MIX_COST_OUT_NORM = 330
MIX_COST_RESIDUAL = 900

_GELU_C0 = math.sqrt(2.0 / math.pi)
_GELU_C1 = _GELU_C0 * 0.044715


def _gelu(x):
    hx = 0.5 * x
    t = jnp.tanh(x * (_GELU_C0 + _GELU_C1 * (x * x)))
    return hx + hx * t


def _shift_rows(x, prev, k):
    ext = jnp.concatenate([prev, x], axis=0)
    n = x.shape[0]
    return ext[SUBLANES - k:SUBLANES - k + n]


def _linear_scan(a, b):
    n = a.shape[0]
    row = lax.broadcasted_iota(jnp.int32, a.shape, 0)
    k = 1
    while k < n:
        if k < SUBLANES:
            keep = row >= k
            a_sh = jnp.where(keep, pltpu.roll(a, k, axis=0), 1.0)
            b_sh = jnp.where(keep, pltpu.roll(b, k, axis=0), 0.0)
        else:
            a_sh = jnp.concatenate([jnp.ones((k, a.shape[1]), F32), a[:n - k]], axis=0)
            b_sh = jnp.concatenate([jnp.zeros((k, a.shape[1]), F32), b[:n - k]], axis=0)
        b = b + a * b_sh
        a = a * a_sh
        k *= 2
    return a, b


def _ada_kernel(c_ref, w_ref, b_ref, o_ref):
    c = c_ref[...]
    c_act = (c * jax.nn.sigmoid(c)).astype(BF16)
    w = w_ref[...].astype(BF16)
    o_ref[...] = jnp.dot(c_act, w, preferred_element_type=F32) + b_ref[...]


def _ada_call(c, w_ada, b_ada):
    bsz, d = c.shape
    n = w_ada.shape[1]
    return pl.pallas_call(
        _ada_kernel,
        grid=(n // ADA_TN,),
        in_specs=[
            pl.BlockSpec((bsz, d), lambda j: (0, 0)),
            pl.BlockSpec((d, ADA_TN), lambda j: (0, j)),
            pl.BlockSpec((1, ADA_TN), lambda j: (0, j)),
        ],
        out_specs=pl.BlockSpec((bsz, ADA_TN), lambda j: (0, j)),
        out_shape=jax.ShapeDtypeStruct((bsz, n), F32),
        compiler_params=pltpu.CompilerParams(
            dimension_semantics=("arbitrary",), vmem_limit_bytes=VMEM_LIMIT_BYTES),
        name="adaln_mod",
    )(c, w_ada, b_ada)


def _stage_a(x_ref, mod_ref, n1_ref, win_ref, hbf_ref, z_ref):
    nblk, _, cols = win_ref.shape

    def prenorm():
        x = x_ref[...]
        inv = lax.rsqrt(jnp.mean(x * x, axis=-1, keepdims=True) + EPS)
        h = x * inv * (n1_ref[...] * (1.0 + mod_ref[1:2, :])) + mod_ref[0:1, :]
        hbf_ref[...] = h.astype(BF16)

    def project(c):
        z_ref[:, c * cols:(c + 1) * cols] = jnp.dot(
            hbf_ref[...], win_ref[c], preferred_element_type=F32)

    return prenorm, [functools.partial(project, c) for c in range(nblk)]


def _stage_b(first, z_ref, lng_ref, lnb_ref, ws_ref, bs_ref, cw_ref, cb_ref, wg_ref, br_ref,
             bi_ref, lam_ref, ong_ref, onl_ref, y_ref, mixbf_ref, halo_ref, carry_ref):
    tm = z_ref.shape[0]
    dg = GMLP_HEADS * GMLP_HEAD_DIM
    dl = y_ref.shape[1] - dg
    nch = tm // CHUNK
    ssq = {"gmlp": jnp.zeros((tm, 1), F32), "lru": jnp.zeros((tm, 1), F32)}

    def head(hd):
        lo = hd * GMLP_HEAD_DIM
        hi = lo + GMLP_HEAD_DIM
        r_i = lax.broadcasted_iota(jnp.int32, (CHUNK, CHUNK), 0)
        c_i = lax.broadcasted_iota(jnp.int32, (CHUNK, CHUNK), 1)
        v = _gelu(z_ref[:, dg + lo:dg + hi])
        mu = jnp.mean(v, axis=-1, keepdims=True)
        dv = v - mu
        var = jnp.mean(dv * dv, axis=-1, keepdims=True)
        vn = (dv * lax.rsqrt(var + EPS) * lng_ref[:, lo:hi] + lnb_ref[:, lo:hi]).astype(BF16)
        rhs = jnp.concatenate([vn[n * CHUNK:(n + 1) * CHUNK] for n in range(nch)], axis=1)
        w = jnp.where(r_i >= c_i, ws_ref[hd], 0.0).astype(BF16)
        m = jnp.dot(w, rhs, preferred_element_type=F32)
        bias = bs_ref[:, lo:hi]
        mixed = jnp.concatenate(
            [m[:, n * CHUNK:(n + 1) * CHUNK] + bias for n in range(nch)], axis=0)
        ya = _gelu(z_ref[:, lo:hi]) * mixed
        y_ref[:, lo:hi] = ya
        ssq["gmlp"] = ssq["gmlp"] + jnp.sum(ya * ya, axis=-1, keepdims=True)

    def lane_group(g):
        lo = g * LANES
        hi = lo + LANES
        xb = z_ref[:, 2 * dg + lo:2 * dg + hi]
        prev = jnp.where(first, 0.0, halo_ref[:, lo:hi])
        halo_ref[:, lo:hi] = xb[tm - SUBLANES:tm]
        kw = cw_ref.shape[0]
        xc = cb_ref[:, lo:hi] + cw_ref[kw - 1:kw, lo:hi] * xb
        for k in range(1, kw):
            xc = xc + cw_ref[kw - 1 - k:kw - k, lo:hi] * _shift_rows(xb, prev, k)
        gates = jnp.dot(xc.astype(BF16), wg_ref[g], preferred_element_type=F32)
        r = jax.nn.sigmoid(gates[:, :LANES] + br_ref[:, lo:hi])
        ig = jax.nn.sigmoid(gates[:, LANES:] + bi_ref[:, lo:hi])
        log_a = (-LRU_C * r) * jax.nn.softplus(-lam_ref[:, lo:hi])
        a = jnp.exp(log_a)
        bterm = jnp.sqrt(jnp.tanh(-log_a) * (1.0 + a * a)) * (ig * xc)
        a_cum, h_loc = _linear_scan(a, bterm)
        h_in = jnp.where(first, 0.0, carry_ref[SUBLANES - 1:SUBLANES, lo:hi])
        hseq = h_loc + a_cum * h_in
        carry_ref[:, lo:hi] = hseq[tm - SUBLANES:tm]
        yb = hseq * _gelu(z_ref[:, 2 * dg + dl + lo:2 * dg + dl + hi])
        y_ref[:, dg + lo:dg + hi] = yb
        ssq["lru"] = ssq["lru"] + jnp.sum(yb * yb, axis=-1, keepdims=True)

    def out_norm():
        inv_a = lax.rsqrt(ssq["gmlp"] * (1.0 / dg) + EPS)
        inv_b = lax.rsqrt(ssq["lru"] * (1.0 / dl) + EPS)
        mixbf_ref[:, :dg] = (y_ref[:, :dg] * inv_a * ong_ref[...]).astype(BF16)
        mixbf_ref[:, dg:] = (y_ref[:, dg:] * inv_b * onl_ref[...]).astype(BF16)

    heads = [functools.partial(head, hd) for hd in range(GMLP_HEADS)]
    groups = [functools.partial(lane_group, g) for g in range(dl // LANES)]
    return heads, groups, out_norm


def _stage_c(mixbf_ref, x_ref, mod_ref, wout_ref, n2_ref, x1_ref, h2_ref):
    outs = []

    def project(c):
        outs.append(jnp.dot(mixbf_ref[...], wout_ref[c], preferred_element_type=F32))

    def residual():
        assert len(outs) == wout_ref.shape[0]
        o = jnp.concatenate(outs, axis=1)
        x1 = x_ref[...] + mod_ref[2:3, :] * o
        x1_ref[...] = x1
        inv2 = lax.rsqrt(jnp.mean(x1 * x1, axis=-1, keepdims=True) + EPS)
        h2 = x1 * inv2 * (n2_ref[...] * (1.0 + mod_ref[4:5, :])) + mod_ref[3:4, :]
        h2_ref[...] = h2.astype(BF16)

    return [functools.partial(project, c) for c in range(wout_ref.shape[0])], residual


def _emit_interleaved(vector_items, matmul_thunks):
    total = sum(cost for cost, _ in vector_items)
    pending = list(matmul_thunks)
    n_mm = len(pending)
    done = 0
    for cost, thunk in vector_items:
        done += cost
        while pending and (n_mm - len(pending)) * total < done * n_mm:
            pending.pop(0)()
        thunk()
    while pending:
        pending.pop(0)()


def _mixer_kernel(xa_ref, xc_ref, moda_ref, modc_ref, n1_ref, win_ref, lng_ref, lnb_ref, ws_ref,
                  bs_ref, cw_ref, cb_ref, wg_ref, br_ref, bi_ref, lam_ref, ong_ref, onl_ref,
                  wout_ref, n2_ref, x1_ref, h2_ref, z0_ref, z1_ref, m0_ref, m1_ref, y_ref, hbf_ref,
                  halo_ref, carry_ref, *, tiles_per_seq, n_tiles):
    s = pl.program_id(0)
    tile_b = jnp.clip(s - 1, 0, n_tiles - 1)
    first = lax.rem(tile_b, tiles_per_seq) == 0

    @pl.when(s == 0)
    def _():
        z1_ref[...] = jnp.zeros_like(z1_ref)
        m1_ref[...] = jnp.zeros_like(m1_ref)
        halo_ref[...] = jnp.zeros_like(halo_ref)
        carry_ref[...] = jnp.zeros_like(carry_ref)

    def step(z_new, z_old, m_new, m_old):
        prenorm, in_proj = _stage_a(xa_ref, moda_ref, n1_ref, win_ref, hbf_ref, z_new)
        heads, groups, out_norm = _stage_b(
            first, z_old, lng_ref, lnb_ref, ws_ref, bs_ref, cw_ref, cb_ref, wg_ref, br_ref, bi_ref,
            lam_ref, ong_ref, onl_ref, y_ref, m_new, halo_ref, carry_ref)
        out_proj, residual = _stage_c(m_old, xc_ref, modc_ref, wout_ref, n2_ref, x1_ref, h2_ref)
        vector_items = (
            [(MIX_COST_PRENORM, prenorm)]
            + [(MIX_COST_HEAD, t) for t in heads]
            + [(MIX_COST_RESIDUAL, residual)]
            + [(MIX_COST_LANE_GROUP, t) for t in groups]
            + [(MIX_COST_OUT_NORM, out_norm)])
        _emit_interleaved(vector_items, out_proj + in_proj)

    @pl.when(lax.rem(s, 2) == 0)
    def _():
        step(z0_ref, z1_ref, m0_ref, m1_ref)

    @pl.when(lax.rem(s, 2) == 1)
    def _():
        step(z1_ref, z0_ref, m1_ref, m0_ref)


def _mixer_call(x, mod, norm1, w_in, ln_g, ln_b, w_s, bs_full, conv_w, conv_b, w_gate,
                b_r, b_i, lam, on_g, on_l, w_out, norm2):
    bsz, s, d = x.shape
    tm = MIX_TM
    d_in = w_in.shape[0] * w_in.shape[2]
    dmix = w_out.shape[1]
    dl = lam.shape[1]
    tps = s // tm
    n = bsz * tps

    def const(shape):
        zeros = (0,) * len(shape)
        return pl.BlockSpec(shape, lambda i: zeros, pipeline_mode=pl.Buffered(1))

    def tile_a(i):
        return jnp.minimum(i, n - 1)

    def tile_c(i):
        return jnp.clip(i - 2, 0, n - 1)

    tok_a = pl.BlockSpec((tm, d), lambda i: (tile_a(i), 0))
    tok_c = pl.BlockSpec((tm, d), lambda i: (tile_c(i), 0))
    x2d = x.reshape(bsz * s, d)
    x1, h2 = pl.pallas_call(
        functools.partial(_mixer_kernel, tiles_per_seq=tps, n_tiles=n),
        grid=(n + 2,),
        in_specs=[
            tok_a, tok_c,
            pl.BlockSpec((None, N_MOD, d), lambda i: (tile_a(i) // tps, 0, 0)),
            pl.BlockSpec((None, N_MOD, d), lambda i: (tile_c(i) // tps, 0, 0)),
            const(norm1.shape), const(w_in.shape), const(ln_g.shape), const(ln_b.shape),
            const(w_s.shape), const(bs_full.shape), const(conv_w.shape), const(conv_b.shape),
            const(w_gate.shape), const(b_r.shape), const(b_i.shape), const(lam.shape),
            const(on_g.shape), const(on_l.shape), const(w_out.shape), const(norm2.shape),
        ],
        out_specs=[tok_c, tok_c],
        out_shape=[jax.ShapeDtypeStruct((bsz * s, d), F32),
                   jax.ShapeDtypeStruct((bsz * s, d), BF16)],
        scratch_shapes=[
            pltpu.VMEM((tm, d_in), F32),
            pltpu.VMEM((tm, d_in), F32),
            pltpu.VMEM((tm, dmix), BF16),
            pltpu.VMEM((tm, dmix), BF16),
            pltpu.VMEM((tm, dmix), F32),
            pltpu.VMEM((tm, d), BF16),
            pltpu.VMEM((SUBLANES, dl), F32),
            pltpu.VMEM((SUBLANES, dl), F32),
        ],
        compiler_params=pltpu.CompilerParams(
            dimension_semantics=("arbitrary",),
            vmem_limit_bytes=VMEM_LIMIT_BYTES),
        name="token_mixer",
    )(x2d, x2d, mod, mod, norm1, w_in, ln_g, ln_b, w_s, bs_full,
      conv_w, conv_b, w_gate, b_r, b_i, lam, on_g, on_l, w_out, norm2)
    return x1.reshape(bsz, s, d), h2.reshape(bsz, s, d)


def _ffn_kernel(h2_ref, x1_ref, mod_ref, wg_ref, wv_ref, cw_ref, cb_ref, wd_ref, nf_ref,
                o_ref, acc_ref, halo_ref):
    tm = h2_ref.shape[0]
    t = pl.program_id(1)
    j = pl.program_id(2)
    h = h2_ref[...]
    kw = cw_ref.shape[0]
    gate = jnp.dot(h, wg_ref[...], preferred_element_type=F32)
    val = jnp.dot(h, wv_ref[...], preferred_element_type=F32)
    prev = jnp.where(t == 0, 0.0, halo_ref[j])
    halo_ref[j] = gate[tm - SUBLANES:tm]
    conv = cb_ref[...] + cw_ref[kw - 1:kw, :] * gate
    for k in range(1, kw):
        conv = conv + cw_ref[kw - 1 - k:kw - k, :] * _shift_rows(gate, prev, k)
    act = (_gelu(conv) * val).astype(BF16)
    contrib = jnp.dot(act, wd_ref[...], preferred_element_type=F32)
    acc_ref[...] = jnp.where(j == 0, contrib, acc_ref[...] + contrib)

    @pl.when(j == pl.num_programs(2) - 1)
    def _():
        xo = x1_ref[...] + mod_ref[N_MOD - 1:N_MOD, :] * acc_ref[...]
        inv = lax.rsqrt(jnp.mean(xo * xo, axis=-1, keepdims=True) + EPS)
        o_ref[...] = xo * inv * nf_ref[...]


def _ffn_call(h2, x1, mod, w_up, conv_w, conv_b, w_down, norm_final):
    bsz, s, d = x1.shape
    f = w_down.shape[0]
    tm, tf = FFN_TM, FFN_TF
    nf = f // tf
    tok = pl.BlockSpec((None, tm, d), lambda b, t, j: (b, t, 0))
    return pl.pallas_call(
        _ffn_kernel,
        grid=(bsz, s // tm, nf),
        in_specs=[
            tok, tok,
            pl.BlockSpec((None, N_MOD, d), lambda b, t, j: (b, 0, 0)),
            pl.BlockSpec((d, tf), lambda b, t, j: (0, j)),
            pl.BlockSpec((d, tf), lambda b, t, j: (0, j + nf)),
            pl.BlockSpec((conv_w.shape[0], tf), lambda b, t, j: (0, j)),
            pl.BlockSpec((1, tf), lambda b, t, j: (0, j)),
            pl.BlockSpec((tf, d), lambda b, t, j: (j, 0)),
            pl.BlockSpec((1, d), lambda b, t, j: (0, 0)),
        ],
        out_specs=tok,
        out_shape=jax.ShapeDtypeStruct((bsz, s, d), F32),
        scratch_shapes=[
            pltpu.VMEM((tm, d), F32),
            pltpu.VMEM((nf, SUBLANES, tf), F32),
        ],
        compiler_params=pltpu.CompilerParams(
            dimension_semantics=("arbitrary", "arbitrary", "arbitrary"),
            vmem_limit_bytes=VMEM_LIMIT_BYTES),
        name="conv_ffn",
    )(h2, x1, mod, w_up, w_up, conv_w, conv_b, w_down, norm_final)


def _col_blocks(w, cols):
    k, n = w.shape
    return w.reshape(k, n // cols, cols).transpose(1, 0, 2)


def _pair_block_diag(w):
    hh, hd, _ = w.shape
    w = w.reshape(hh // 2, 2, hd, hd)
    z = jnp.zeros((hh // 2, hd, hd), w.dtype)
    top = jnp.concatenate([w[:, 0], z], axis=2)
    bot = jnp.concatenate([z, w[:, 1]], axis=2)
    return jnp.concatenate([top, bot], axis=1)


def kernel(x, c, w_ada, b_ada, norm1, w_in, gmlp_ln_g, gmlp_ln_b, gmlp_w_s, gmlp_b_s, lru_conv_w, lru_conv_b, lru_w_r, lru_b_r, lru_w_i, lru_b_i, lru_lambda, out_norm_gmlp, out_norm_lru, w_out, norm2, w_up, ffn_conv_w, ffn_conv_b, w_down, norm_final):
    depth = w_ada.shape[0]
    assert depth == 1, "the final RMSNorm is fused into the FFN kernel of a single layer"
    bsz, s, d = x.shape
    for l in range(depth):
        mod = _ada_call(c, w_ada[l], b_ada[l][None, :]).reshape(bsz, N_MOD, d)
        bs_full = jnp.repeat(gmlp_b_s[l].T, GMLP_HEAD_DIM, axis=1)
        w_gate = jnp.concatenate(
            [_pair_block_diag(lru_w_r[l]), _pair_block_diag(lru_w_i[l])], axis=2).astype(BF16)
        x1, h2 = _mixer_call(
            x, mod, norm1[l][None, :], _col_blocks(w_in[l].astype(BF16), MIX_WCOLS),
            gmlp_ln_g[l][None, :], gmlp_ln_b[l][None, :], gmlp_w_s[l], bs_full,
            lru_conv_w[l], lru_conv_b[l][None, :], w_gate,
            lru_b_r[l][None, :], lru_b_i[l][None, :], lru_lambda[l][None, :],
            out_norm_gmlp[l][None, :], out_norm_lru[l][None, :],
            _col_blocks(w_out[l].astype(BF16), MIX_WCOLS), norm2[l][None, :])
        x = _ffn_call(h2, x1, mod, w_up[l].astype(BF16), ffn_conv_w[l],
                      ffn_conv_b[l][None, :], w_down[l].astype(BF16), norm_final[None, :])
    return x
```

```python
import functools
import math

import jax
import jax.numpy as jnp
from jax import lax
from jax.experimental import pallas as pl
from jax.experimental.pallas import tpu as pltpu

F32 = jnp.float32
BF16 = jnp.bfloat16

CHUNK = 128
GMLP_HEADS = 8
GMLP_HEAD_DIM = 128
LRU_HEAD_DIM = 64
LRU_C = 8.0
N_MOD = 6
EPS = 1e-6

LANES = 128
SUBLANES = 8
VMEM_LIMIT_BYTES = 56 * 1024 * 1024

ADA_TN = 1024
MIX_TM = 256
MIX_WCOLS = 512
FFN_TM = 512
FFN_TF = 512

_GELU_C0 = math.sqrt(2.0 / math.pi)
_GELU_C1 = _GELU_C0 * 0.044715


def _gelu(x):
    hx = 0.5 * x
    t = jnp.tanh(x * (_GELU_C0 + _GELU_C1 * (x * x)))
    return hx + hx * t


def _shift_rows(x, prev, k):
    ext = jnp.concatenate([prev, x], axis=0)
    n = x.shape[0]
    return ext[SUBLANES - k:SUBLANES - k + n]


def _linear_scan(a, b):
    n = a.shape[0]
    row = lax.broadcasted_iota(jnp.int32, a.shape, 0)
    k = 1
    while k < n:
        if k < SUBLANES:
            keep = row >= k
            a_sh = jnp.where(keep, pltpu.roll(a, k, axis=0), 1.0)
            b_sh = jnp.where(keep, pltpu.roll(b, k, axis=0), 0.0)
        else:
            a_sh = jnp.concatenate([jnp.ones((k, a.shape[1]), F32), a[:n - k]], axis=0)
            b_sh = jnp.concatenate([jnp.zeros((k, a.shape[1]), F32), b[:n - k]], axis=0)
        b = b + a * b_sh
        a = a * a_sh
        k *= 2
    return a, b


def _ada_kernel(c_ref, w_ref, b_ref, o_ref):
    c = c_ref[...]
    c_act = (c * jax.nn.sigmoid(c)).astype(BF16)
    w = w_ref[...].astype(BF16)
    o_ref[...] = jnp.dot(c_act, w, preferred_element_type=F32) + b_ref[...]


def _ada_call(c, w_ada, b_ada):
    bsz, d = c.shape
    n = w_ada.shape[1]
    return pl.pallas_call(
        _ada_kernel,
        grid=(n // ADA_TN,),
        in_specs=[
            pl.BlockSpec((bsz, d), lambda j: (0, 0)),
            pl.BlockSpec((d, ADA_TN), lambda j: (0, j)),
            pl.BlockSpec((1, ADA_TN), lambda j: (0, j)),
        ],
        out_specs=pl.BlockSpec((bsz, ADA_TN), lambda j: (0, j)),
        out_shape=jax.ShapeDtypeStruct((bsz, n), F32),
        compiler_params=pltpu.CompilerParams(
            dimension_semantics=("arbitrary",), vmem_limit_bytes=VMEM_LIMIT_BYTES),
        name="adaln_mod",
    )(c, w_ada, b_ada)


def _mixer_kernel(*refs, n_in_blocks, n_out_blocks):
    refs = list(refs)
    x_ref, mod_ref, n1_ref = refs[:3]
    win_refs = refs[3:3 + n_in_blocks]
    (lng_ref, lnb_ref, ws_ref, bs_ref, cw_ref, cb_ref, wg_ref, br_ref, bi_ref, lam_ref,
     ong_ref, onl_ref) = refs[3 + n_in_blocks:15 + n_in_blocks]
    wout_refs = refs[15 + n_in_blocks:15 + n_in_blocks + n_out_blocks]
    (n2_ref, wup_ref, x1_ref, h2_ref, wup_bf_ref, z_ref, y_ref, hbf_ref, mixbf_ref, halo_ref,
     carry_ref) = refs[15 + n_in_blocks + n_out_blocks:]

    tm, d = x_ref.shape
    dg = GMLP_HEADS * GMLP_HEAD_DIM
    dl = y_ref.shape[1] - dg
    nch = tm // CHUNK

    @pl.when(pl.program_id(1) == 0)
    def _():
        halo_ref[...] = jnp.zeros_like(halo_ref)
        carry_ref[...] = jnp.zeros_like(carry_ref)

    wup_bf_ref[...] = wup_ref[...].astype(BF16)

    x = x_ref[...]
    inv = lax.rsqrt(jnp.mean(x * x, axis=-1, keepdims=True) + EPS)
    h = x * inv * (n1_ref[...] * (1.0 + mod_ref[1:2, :])) + mod_ref[0:1, :]
    hbf_ref[...] = h.astype(BF16)
    wcols = win_refs[0].shape[1]
    for c, w_ref in enumerate(win_refs):
        z_ref[:, c * wcols:(c + 1) * wcols] = jnp.dot(
            hbf_ref[...], w_ref[...], preferred_element_type=F32)

    r_i = lax.broadcasted_iota(jnp.int32, (CHUNK, CHUNK), 0)
    c_i = lax.broadcasted_iota(jnp.int32, (CHUNK, CHUNK), 1)
    causal = r_i >= c_i
    ssq = jnp.zeros((tm, 1), F32)
    for hd in range(GMLP_HEADS):
        lo = hd * GMLP_HEAD_DIM
        hi = lo + GMLP_HEAD_DIM
        v = _gelu(z_ref[:, dg + lo:dg + hi])
        mu = jnp.mean(v, axis=-1, keepdims=True)
        dv = v - mu
        var = jnp.mean(dv * dv, axis=-1, keepdims=True)
        vn = (dv * lax.rsqrt(var + EPS) * lng_ref[:, lo:hi] + lnb_ref[:, lo:hi]).astype(BF16)
        rhs = jnp.concatenate([vn[n * CHUNK:(n + 1) * CHUNK] for n in range(nch)], axis=1)
        w = jnp.where(causal, ws_ref[hd], 0.0).astype(BF16)
        m = jnp.dot(w, rhs, preferred_element_type=F32)
        bias = bs_ref[:, lo:hi]
        mixed = jnp.concatenate(
            [m[:, n * CHUNK:(n + 1) * CHUNK] + bias for n in range(nch)], axis=0)
        ya = _gelu(z_ref[:, lo:hi]) * mixed
        y_ref[:, lo:hi] = ya
        ssq = ssq + jnp.sum(ya * ya, axis=-1, keepdims=True)
    inv_a = lax.rsqrt(ssq * (1.0 / dg) + EPS)

    ssq = jnp.zeros((tm, 1), F32)
    for g in range(dl // LANES):
        lo = g * LANES
        hi = lo + LANES
        xb = z_ref[:, 2 * dg + lo:2 * dg + hi]
        prev = halo_ref[:, lo:hi]
        halo_ref[:, lo:hi] = xb[tm - SUBLANES:tm]
        kw = cw_ref.shape[0]
        xc = cb_ref[:, lo:hi] + cw_ref[kw - 1:kw, lo:hi] * xb
        for k in range(1, kw):
            xc = xc + cw_ref[kw - 1 - k:kw - k, lo:hi] * _shift_rows(xb, prev, k)
        gates = jnp.dot(xc.astype(BF16), wg_ref[g], preferred_element_type=F32)
        r = jax.nn.sigmoid(gates[:, :LANES] + br_ref[:, lo:hi])
        ig = jax.nn.sigmoid(gates[:, LANES:] + bi_ref[:, lo:hi])
        log_a = (-LRU_C * r) * jax.nn.softplus(-lam_ref[:, lo:hi])
        a = jnp.exp(log_a)
        bterm = jnp.sqrt(jnp.tanh(-log_a) * (1.0 + a * a)) * (ig * xc)
        a_cum, h_loc = _linear_scan(a, bterm)
        hseq = h_loc + a_cum * carry_ref[SUBLANES - 1:SUBLANES, lo:hi]
        carry_ref[:, lo:hi] = hseq[tm - SUBLANES:tm]
        yb = hseq * _gelu(z_ref[:, 2 * dg + dl + lo:2 * dg + dl + hi])
        y_ref[:, dg + lo:dg + hi] = yb
        ssq = ssq + jnp.sum(yb * yb, axis=-1, keepdims=True)
    inv_b = lax.rsqrt(ssq * (1.0 / dl) + EPS)

    mixbf_ref[:, :dg] = (y_ref[:, :dg] * inv_a * ong_ref[...]).astype(BF16)
    mixbf_ref[:, dg:] = (y_ref[:, dg:] * inv_b * onl_ref[...]).astype(BF16)
    o = jnp.concatenate(
        [jnp.dot(mixbf_ref[...], w_ref[...], preferred_element_type=F32) for w_ref in wout_refs],
        axis=1)
    x1 = x_ref[...] + mod_ref[2:3, :] * o
    x1_ref[...] = x1
    inv2 = lax.rsqrt(jnp.mean(x1 * x1, axis=-1, keepdims=True) + EPS)
    h2 = x1 * inv2 * (n2_ref[...] * (1.0 + mod_ref[4:5, :])) + mod_ref[3:4, :]
    h2_ref[...] = h2.astype(BF16)


def _mixer_call(x, mod, norm1, w_in, ln_g, ln_b, w_s, bs_full, conv_w, conv_b, w_gate,
                b_r, b_i, lam, on_g, on_l, w_out, norm2, w_up):
    bsz, s, d = x.shape
    tm = MIX_TM
    d_in = w_in.shape[1]
    dmix = w_out.shape[0]
    dl = lam.shape[1]
    tps = s // tm
    n_in_blocks = d_in // MIX_WCOLS
    n_out_blocks = w_out.shape[1] // MIX_WCOLS
    up_rows = w_up.shape[0] // (bsz * tps)

    def const(shape):
        zeros = (0,) * len(shape)
        return pl.BlockSpec(shape, lambda b, t: zeros, pipeline_mode=pl.Buffered(1))

    def col_block(w, c):
        return pl.BlockSpec((w.shape[0], MIX_WCOLS), lambda b, t: (0, c),
                            pipeline_mode=pl.Buffered(1))

    tok = pl.BlockSpec((None, tm, d), lambda b, t: (b, t, 0))
    up_slab = pl.BlockSpec((up_rows, w_up.shape[1]), lambda b, t: (b * tps + t, 0))
    in_specs = (
        [tok, pl.BlockSpec((None, N_MOD, d), lambda b, t: (b, 0, 0)), const(norm1.shape)]
        + [col_block(w_in, c) for c in range(n_in_blocks)]
        + [const(a.shape) for a in (ln_g, ln_b, w_s, bs_full, conv_w, conv_b, w_gate, b_r, b_i,
                                    lam, on_g, on_l)]
        + [col_block(w_out, c) for c in range(n_out_blocks)]
        + [const(norm2.shape), up_slab])
    operands = (
        [x, mod, norm1] + [w_in] * n_in_blocks
        + [ln_g, ln_b, w_s, bs_full, conv_w, conv_b, w_gate, b_r, b_i, lam, on_g, on_l]
        + [w_out] * n_out_blocks + [norm2, w_up])
    return pl.pallas_call(
        functools.partial(_mixer_kernel, n_in_blocks=n_in_blocks, n_out_blocks=n_out_blocks),
        grid=(bsz, tps),
        in_specs=in_specs,
        out_specs=[tok, tok, up_slab],
        out_shape=[jax.ShapeDtypeStruct((bsz, s, d), F32),
                   jax.ShapeDtypeStruct((bsz, s, d), BF16),
                   jax.ShapeDtypeStruct(w_up.shape, BF16)],
        scratch_shapes=[
            pltpu.VMEM((tm, d_in), F32),
            pltpu.VMEM((tm, dmix), F32),
            pltpu.VMEM((tm, d), BF16),
            pltpu.VMEM((tm, dmix), BF16),
            pltpu.VMEM((SUBLANES, dl), F32),
            pltpu.VMEM((SUBLANES, dl), F32),
        ],
        compiler_params=pltpu.CompilerParams(
            dimension_semantics=("arbitrary", "arbitrary"),
            vmem_limit_bytes=VMEM_LIMIT_BYTES),
        name="token_mixer",
    )(*operands)


def _ffn_kernel(h2_ref, x1_ref, mod_ref, wg_ref, wv_ref, cw_ref, cb_ref, wd_ref, nf_ref,
                o_ref, acc_ref, halo_ref):
    tm = h2_ref.shape[0]
    t = pl.program_id(1)
    j = pl.program_id(2)
    h = h2_ref[...]
    kw = cw_ref.shape[0]
    gate = jnp.dot(h, wg_ref[...], preferred_element_type=F32)
    val = jnp.dot(h, wv_ref[...], preferred_element_type=F32)
    prev = jnp.where(t == 0, 0.0, halo_ref[j])
    halo_ref[j] = gate[tm - SUBLANES:tm]
    conv = cb_ref[...] + cw_ref[kw - 1:kw, :] * gate
    for k in range(1, kw):
        conv = conv + cw_ref[kw - 1 - k:kw - k, :] * _shift_rows(gate, prev, k)
    act = (_gelu(conv) * val).astype(BF16)
    contrib = jnp.dot(act, wd_ref[...], preferred_element_type=F32)
    acc_ref[...] = jnp.where(j == 0, contrib, acc_ref[...] + contrib)

    @pl.when(j == pl.num_programs(2) - 1)
    def _():
        xo = x1_ref[...] + mod_ref[N_MOD - 1:N_MOD, :] * acc_ref[...]
        inv = lax.rsqrt(jnp.mean(xo * xo, axis=-1, keepdims=True) + EPS)
        o_ref[...] = xo * inv * nf_ref[...]


def _ffn_call(h2, x1, mod, w_up, conv_w, conv_b, w_down, norm_final):
    bsz, s, d = x1.shape
    f = w_down.shape[0]
    tm, tf = FFN_TM, FFN_TF
    nf = f // tf
    tok = pl.BlockSpec((None, tm, d), lambda b, t, j: (b, t, 0))
    return pl.pallas_call(
        _ffn_kernel,
        grid=(bsz, s // tm, nf),
        in_specs=[
            tok, tok,
            pl.BlockSpec((None, N_MOD, d), lambda b, t, j: (b, 0, 0)),
            pl.BlockSpec((d, tf), lambda b, t, j: (0, j)),
            pl.BlockSpec((d, tf), lambda b, t, j: (0, j + nf)),
            pl.BlockSpec((conv_w.shape[0], tf), lambda b, t, j: (0, j)),
            pl.BlockSpec((1, tf), lambda b, t, j: (0, j)),
            pl.BlockSpec((tf, d), lambda b, t, j: (j, 0)),
            pl.BlockSpec((1, d), lambda b, t, j: (0, 0)),
        ],
        out_specs=tok,
        out_shape=jax.ShapeDtypeStruct((bsz, s, d), F32),
        scratch_shapes=[
            pltpu.VMEM((tm, d), F32),
            pltpu.VMEM((nf, SUBLANES, tf), F32),
        ],
        compiler_params=pltpu.CompilerParams(
            dimension_semantics=("arbitrary", "arbitrary", "arbitrary"),
            vmem_limit_bytes=VMEM_LIMIT_BYTES),
        name="conv_ffn",
    )(h2, x1, mod, w_up, w_up, conv_w, conv_b, w_down, norm_final)


def _pair_block_diag(w):
    hh, hd, _ = w.shape
    w = w.reshape(hh // 2, 2, hd, hd)
    z = jnp.zeros((hh // 2, hd, hd), w.dtype)
    top = jnp.concatenate([w[:, 0], z], axis=2)
    bot = jnp.concatenate([z, w[:, 1]], axis=2)
    return jnp.concatenate([top, bot], axis=1)


def kernel(x, c, w_ada, b_ada, norm1, w_in, gmlp_ln_g, gmlp_ln_b, gmlp_w_s, gmlp_b_s, lru_conv_w, lru_conv_b, lru_w_r, lru_b_r, lru_w_i, lru_b_i, lru_lambda, out_norm_gmlp, out_norm_lru, w_out, norm2, w_up, ffn_conv_w, ffn_conv_b, w_down, norm_final):
    depth = w_ada.shape[0]
    assert depth == 1, "the final RMSNorm is fused into the FFN kernel of a single layer"
    bsz, s, d = x.shape
    for l in range(depth):
        mod = _ada_call(c, w_ada[l], b_ada[l][None, :]).reshape(bsz, N_MOD, d)
        bs_full = jnp.repeat(gmlp_b_s[l].T, GMLP_HEAD_DIM, axis=1)
        w_gate = jnp.concatenate(
            [_pair_block_diag(lru_w_r[l]), _pair_block_diag(lru_w_i[l])], axis=2).astype(BF16)
        x1, h2, w_up_bf = _mixer_call(
            x, mod, norm1[l][None, :], w_in[l].astype(BF16),
            gmlp_ln_g[l][None, :], gmlp_ln_b[l][None, :], gmlp_w_s[l], bs_full,
            lru_conv_w[l], lru_conv_b[l][None, :], w_gate,
            lru_b_r[l][None, :], lru_b_i[l][None, :], lru_lambda[l][None, :],
            out_norm_gmlp[l][None, :], out_norm_lru[l][None, :],
            w_out[l].astype(BF16), norm2[l][None, :], w_up[l])
        x = _ffn_call(h2, x1, mod, w_up_bf, ffn_conv_w[l],
                      ffn_conv_b[l][None, :], w_down[l].astype(BF16), norm_final[None, :])
    return x
```

```python
import functools
import math

import jax
import jax.numpy as jnp
from jax import lax
from jax.experimental import pallas as pl
from jax.experimental.pallas import tpu as pltpu

F32 = jnp.float32
BF16 = jnp.bfloat16

CHUNK = 128
GMLP_HEADS = 8
GMLP_HEAD_DIM = 128
LRU_HEAD_DIM = 64
LRU_C = 8.0
N_MOD = 6
EPS = 1e-6

LANES = 128
SUBLANES = 8
VMEM_LIMIT_BYTES = 56 * 1024 * 1024

ADA_TN = 1024
MIX_TM = 256
MIX_WCOLS = 512
FFN_TM = 512
FFN_TF = 512

_GELU_C0 = math.sqrt(2.0 / math.pi)
_GELU_C1 = _GELU_C0 * 0.044715


def _gelu(x):
    hx = 0.5 * x
    t = jnp.tanh(x * (_GELU_C0 + _GELU_C1 * (x * x)))
    return hx + hx * t


def _shift_rows(x, prev, k):
    ext = jnp.concatenate([prev, x], axis=0)
    n = x.shape[0]
    return ext[SUBLANES - k:SUBLANES - k + n]


def _scan_steps(a, b, row, shifts):
    n = a.shape[0]
    for k in shifts:
        if k < SUBLANES:
            keep = row >= k
            a_sh = jnp.where(keep, pltpu.roll(a, k, axis=0), 1.0)
            b_sh = jnp.where(keep, pltpu.roll(b, k, axis=0), 0.0)
        else:
            a_sh = jnp.concatenate([jnp.ones((k, a.shape[1]), F32), a[:n - k]], axis=0)
            b_sh = jnp.concatenate([jnp.zeros((k, a.shape[1]), F32), b[:n - k]], axis=0)
        b = b + a * b_sh
        a = a * a_sh
    return a, b


def _linear_recurrence(a, b, h_in, ga_ref, gb_ref):
    n, lanes = a.shape
    ng = n // SUBLANES
    a = a.reshape(ng, SUBLANES, lanes)
    b = b.reshape(ng, SUBLANES, lanes)
    sub = lax.broadcasted_iota(jnp.int32, (1, SUBLANES, lanes), 1)
    for k in (1, 2, 4):
        keep = sub >= k
        a_sh = jnp.where(keep, pltpu.roll(a, k, axis=1), 1.0)
        b_sh = jnp.where(keep, pltpu.roll(b, k, axis=1), 0.0)
        b = b + a * b_sh
        a = a * a_sh
    a = a.reshape(n, lanes)
    b = b.reshape(n, lanes)
    ga_ref[...] = a
    gb_ref[...] = b
    ta = ga_ref[pl.ds(SUBLANES - 1, ng, stride=SUBLANES), :]
    tb = gb_ref[pl.ds(SUBLANES - 1, ng, stride=SUBLANES), :]
    grow = lax.broadcasted_iota(jnp.int32, ta.shape, 0)
    shifts = []
    k = 1
    while k < ng:
        shifts.append(k)
        k *= 2
    ta, tb = _scan_steps(ta, tb, grow, shifts)
    h_end = tb + ta * h_in
    carry_in = jnp.where(grow == 0, h_in, pltpu.roll(h_end, 1, axis=0))
    carry_rows = jnp.concatenate(
        [jnp.broadcast_to(carry_in[g:g + 1, :], (SUBLANES, lanes)) for g in range(ng)], axis=0)
    return b + a * carry_rows


def _ada_kernel(c_ref, w_ref, b_ref, o_ref):
    c = c_ref[...]
    c_act = (c * jax.nn.sigmoid(c)).astype(BF16)
    w = w_ref[...].astype(BF16)
    o_ref[...] = jnp.dot(c_act, w, preferred_element_type=F32) + b_ref[...]


def _ada_call(c, w_ada, b_ada):
    bsz, d = c.shape
    n = w_ada.shape[1]
    return pl.pallas_call(
        _ada_kernel,
        grid=(n // ADA_TN,),
        in_specs=[
            pl.BlockSpec((bsz, d), lambda j: (0, 0)),
            pl.BlockSpec((d, ADA_TN), lambda j: (0, j)),
            pl.BlockSpec((1, ADA_TN), lambda j: (0, j)),
        ],
        out_specs=pl.BlockSpec((bsz, ADA_TN), lambda j: (0, j)),
        out_shape=jax.ShapeDtypeStruct((bsz, n), F32),
        compiler_params=pltpu.CompilerParams(
            dimension_semantics=("arbitrary",), vmem_limit_bytes=VMEM_LIMIT_BYTES),
        name="adaln_mod",
    )(c, w_ada, b_ada)


def _mixer_kernel(*refs, n_in_blocks, n_out_blocks):
    refs = list(refs)
    x_ref, mod_ref, n1_ref = refs[:3]
    win_refs = refs[3:3 + n_in_blocks]
    (lng_ref, lnb_ref, ws_ref, bs_ref, cw_ref, cb_ref, wg_ref, br_ref, bi_ref, lam_ref,
     ong_ref, onl_ref) = refs[3 + n_in_blocks:15 + n_in_blocks]
    wout_refs = refs[15 + n_in_blocks:15 + n_in_blocks + n_out_blocks]
    (n2_ref, wup_ref, wdn_ref, x1_ref, h2_ref, wup_bf_ref, wdn_bf_ref, z_ref, y_ref, hbf_ref,
     mixbf_ref, halo_ref, carry_ref, ga_ref, gb_ref) = refs[15 + n_in_blocks + n_out_blocks:]

    tm, d = x_ref.shape
    dg = GMLP_HEADS * GMLP_HEAD_DIM
    dl = y_ref.shape[1] - dg
    nch = tm // CHUNK

    @pl.when(pl.program_id(1) == 0)
    def _():
        halo_ref[...] = jnp.zeros_like(halo_ref)
        carry_ref[...] = jnp.zeros_like(carry_ref)

    wup_bf_ref[...] = wup_ref[...].astype(BF16)
    wdn_bf_ref[...] = wdn_ref[...].astype(BF16)

    x = x_ref[...]
    inv = lax.rsqrt(jnp.mean(x * x, axis=-1, keepdims=True) + EPS)
    h = x * inv * (n1_ref[...] * (1.0 + mod_ref[1:2, :])) + mod_ref[0:1, :]
    hbf_ref[...] = h.astype(BF16)
    wcols = win_refs[0].shape[1]
    for c, w_ref in enumerate(win_refs):
        z_ref[:, c * wcols:(c + 1) * wcols] = jnp.dot(
            hbf_ref[...], w_ref[...], preferred_element_type=F32)

    r_i = lax.broadcasted_iota(jnp.int32, (CHUNK, CHUNK), 0)
    c_i = lax.broadcasted_iota(jnp.int32, (CHUNK, CHUNK), 1)
    causal = r_i >= c_i
    ssq = jnp.zeros((tm, 1), F32)
    for hd in range(GMLP_HEADS):
        lo = hd * GMLP_HEAD_DIM
        hi = lo + GMLP_HEAD_DIM
        v = _gelu(z_ref[:, dg + lo:dg + hi])
        mu = jnp.mean(v, axis=-1, keepdims=True)
        dv = v - mu
        var = jnp.mean(dv * dv, axis=-1, keepdims=True)
        vn = (dv * lax.rsqrt(var + EPS) * lng_ref[:, lo:hi] + lnb_ref[:, lo:hi]).astype(BF16)
        rhs = jnp.concatenate([vn[n * CHUNK:(n + 1) * CHUNK] for n in range(nch)], axis=1)
        w = jnp.where(causal, ws_ref[hd], 0.0).astype(BF16)
        m = jnp.dot(w, rhs, preferred_element_type=F32)
        bias = bs_ref[:, lo:hi]
        mixed = jnp.concatenate(
            [m[:, n * CHUNK:(n + 1) * CHUNK] + bias for n in range(nch)], axis=0)
        ya = _gelu(z_ref[:, lo:hi]) * mixed
        y_ref[:, lo:hi] = ya
        ssq = ssq + jnp.sum(ya * ya, axis=-1, keepdims=True)
    inv_a = lax.rsqrt(ssq * (1.0 / dg) + EPS)

    ssq = jnp.zeros((tm, 1), F32)
    for g in range(dl // LANES):
        lo = g * LANES
        hi = lo + LANES
        xb = z_ref[:, 2 * dg + lo:2 * dg + hi]
        prev = halo_ref[:, lo:hi]
        halo_ref[:, lo:hi] = xb[tm - SUBLANES:tm]
        kw = cw_ref.shape[0]
        xc = cb_ref[:, lo:hi] + cw_ref[kw - 1:kw, lo:hi] * xb
        for k in range(1, kw):
            xc = xc + cw_ref[kw - 1 - k:kw - k, lo:hi] * _shift_rows(xb, prev, k)
        gates = jnp.dot(xc.astype(BF16), wg_ref[g], preferred_element_type=F32)
        r = jax.nn.sigmoid(gates[:, :LANES] + br_ref[:, lo:hi])
        ig = jax.nn.sigmoid(gates[:, LANES:] + bi_ref[:, lo:hi])
        log_a = (-LRU_C * r) * jax.nn.softplus(-lam_ref[:, lo:hi])
        a = jnp.exp(log_a)
        bterm = jnp.sqrt(jnp.tanh(-log_a) * (1.0 + a * a)) * (ig * xc)
        hseq = _linear_recurrence(a, bterm, carry_ref[SUBLANES - 1:SUBLANES, lo:hi], ga_ref, gb_ref)
        carry_ref[:, lo:hi] = hseq[tm - SUBLANES:tm]
        yb = hseq * _gelu(z_ref[:, 2 * dg + dl + lo:2 * dg + dl + hi])
        y_ref[:, dg + lo:dg + hi] = yb
        ssq = ssq + jnp.sum(yb * yb, axis=-1, keepdims=True)
    inv_b = lax.rsqrt(ssq * (1.0 / dl) + EPS)

    mixbf_ref[:, :dg] = (y_ref[:, :dg] * inv_a * ong_ref[...]).astype(BF16)
    mixbf_ref[:, dg:] = (y_ref[:, dg:] * inv_b * onl_ref[...]).astype(BF16)
    o = jnp.concatenate(
        [jnp.dot(mixbf_ref[...], w_ref[...], preferred_element_type=F32) for w_ref in wout_refs],
        axis=1)
    x1 = x_ref[...] + mod_ref[2:3, :] * o
    x1_ref[...] = x1
    inv2 = lax.rsqrt(jnp.mean(x1 * x1, axis=-1, keepdims=True) + EPS)
    h2 = x1 * inv2 * (n2_ref[...] * (1.0 + mod_ref[4:5, :])) + mod_ref[3:4, :]
    h2_ref[...] = h2.astype(BF16)


def _mixer_call(x, mod, norm1, w_in, ln_g, ln_b, w_s, bs_full, conv_w, conv_b, w_gate,
                b_r, b_i, lam, on_g, on_l, w_out, norm2, w_up, w_down):
    bsz, s, d = x.shape
    tm = MIX_TM
    d_in = w_in.shape[1]
    dmix = w_out.shape[0]
    dl = lam.shape[1]
    tps = s // tm
    n_in_blocks = d_in // MIX_WCOLS
    n_out_blocks = w_out.shape[1] // MIX_WCOLS
    n_steps = bsz * tps

    def row_slab(w):
        return pl.BlockSpec((w.shape[0] // n_steps, w.shape[1]), lambda b, t: (b * tps + t, 0))

    def const(shape):
        zeros = (0,) * len(shape)
        return pl.BlockSpec(shape, lambda b, t: zeros, pipeline_mode=pl.Buffered(1))

    def col_block(w, c):
        return pl.BlockSpec((w.shape[0], MIX_WCOLS), lambda b, t: (0, c),
                            pipeline_mode=pl.Buffered(1))

    tok = pl.BlockSpec((None, tm, d), lambda b, t: (b, t, 0))
    in_specs = (
        [tok, pl.BlockSpec((None, N_MOD, d), lambda b, t: (b, 0, 0)), const(norm1.shape)]
        + [col_block(w_in, c) for c in range(n_in_blocks)]
        + [const(a.shape) for a in (ln_g, ln_b, w_s, bs_full, conv_w, conv_b, w_gate, b_r, b_i,
                                    lam, on_g, on_l)]
        + [col_block(w_out, c) for c in range(n_out_blocks)]
        + [const(norm2.shape), row_slab(w_up), row_slab(w_down)])
    operands = (
        [x, mod, norm1] + [w_in] * n_in_blocks
        + [ln_g, ln_b, w_s, bs_full, conv_w, conv_b, w_gate, b_r, b_i, lam, on_g, on_l]
        + [w_out] * n_out_blocks + [norm2, w_up, w_down])
    return pl.pallas_call(
        functools.partial(_mixer_kernel, n_in_blocks=n_in_blocks, n_out_blocks=n_out_blocks),
        grid=(bsz, tps),
        in_specs=in_specs,
        out_specs=[tok, tok, row_slab(w_up), row_slab(w_down)],
        out_shape=[jax.ShapeDtypeStruct((bsz, s, d), F32),
                   jax.ShapeDtypeStruct((bsz, s, d), BF16),
                   jax.ShapeDtypeStruct(w_up.shape, BF16),
                   jax.ShapeDtypeStruct(w_down.shape, BF16)],
        scratch_shapes=[
            pltpu.VMEM((tm, d_in), F32),
            pltpu.VMEM((tm, dmix), F32),
            pltpu.VMEM((tm, d), BF16),
            pltpu.VMEM((tm, dmix), BF16),
            pltpu.VMEM((SUBLANES, dl), F32),
            pltpu.VMEM((SUBLANES, dl), F32),
            pltpu.VMEM((tm, LANES), F32),
            pltpu.VMEM((tm, LANES), F32),
        ],
        compiler_params=pltpu.CompilerParams(
            dimension_semantics=("arbitrary", "arbitrary"),
            vmem_limit_bytes=VMEM_LIMIT_BYTES),
        name="token_mixer",
    )(*operands)


def _ffn_kernel(h2_ref, x1_ref, mod_ref, wg_ref, wv_ref, cw_ref, cb_ref, wd_ref, nf_ref,
                o_ref, acc_ref, halo_ref):
    tm = h2_ref.shape[0]
    t = pl.program_id(1)
    j = pl.program_id(2)
    h = h2_ref[...]
    kw = cw_ref.shape[0]
    gate = jnp.dot(h, wg_ref[...], preferred_element_type=F32)
    val = jnp.dot(h, wv_ref[...], preferred_element_type=F32)
    prev = jnp.where(t == 0, 0.0, halo_ref[j])
    halo_ref[j] = gate[tm - SUBLANES:tm]
    conv = cb_ref[...] + cw_ref[kw - 1:kw, :] * gate
    for k in range(1, kw):
        conv = conv + cw_ref[kw - 1 - k:kw - k, :] * _shift_rows(gate, prev, k)
    act = (_gelu(conv) * val).astype(BF16)
    contrib = jnp.dot(act, wd_ref[...], preferred_element_type=F32)
    acc_ref[...] = jnp.where(j == 0, contrib, acc_ref[...] + contrib)

    @pl.when(j == pl.num_programs(2) - 1)
    def _():
        xo = x1_ref[...] + mod_ref[N_MOD - 1:N_MOD, :] * acc_ref[...]
        inv = lax.rsqrt(jnp.mean(xo * xo, axis=-1, keepdims=True) + EPS)
        o_ref[...] = xo * inv * nf_ref[...]


def _ffn_call(h2, x1, mod, w_up, conv_w, conv_b, w_down, norm_final):
    bsz, s, d = x1.shape
    f = w_down.shape[0]
    tm, tf = FFN_TM, FFN_TF
    nf = f // tf
    tok = pl.BlockSpec((None, tm, d), lambda b, t, j: (b, t, 0))
    return pl.pallas_call(
        _ffn_kernel,
        grid=(bsz, s // tm, nf),
        in_specs=[
            tok, tok,
            pl.BlockSpec((None, N_MOD, d), lambda b, t, j: (b, 0, 0)),
            pl.BlockSpec((d, tf), lambda b, t, j: (0, j)),
            pl.BlockSpec((d, tf), lambda b, t, j: (0, j + nf)),
            pl.BlockSpec((conv_w.shape[0], tf), lambda b, t, j: (0, j)),
            pl.BlockSpec((1, tf), lambda b, t, j: (0, j)),
            pl.BlockSpec((tf, d), lambda b, t, j: (j, 0)),
            pl.BlockSpec((1, d), lambda b, t, j: (0, 0)),
        ],
        out_specs=tok,
        out_shape=jax.ShapeDtypeStruct((bsz, s, d), F32),
        scratch_shapes=[
            pltpu.VMEM((tm, d), F32),
            pltpu.VMEM((nf, SUBLANES, tf), F32),
        ],
        compiler_params=pltpu.CompilerParams(
            dimension_semantics=("arbitrary", "arbitrary", "arbitrary"),
            vmem_limit_bytes=VMEM_LIMIT_BYTES),
        name="conv_ffn",
    )(h2, x1, mod, w_up, w_up, conv_w, conv_b, w_down, norm_final)


def _pair_block_diag(w):
    hh, hd, _ = w.shape
    w = w.reshape(hh // 2, 2, hd, hd)
    z = jnp.zeros((hh // 2, hd, hd), w.dtype)
    top = jnp.concatenate([w[:, 0], z], axis=2)
    bot = jnp.concatenate([z, w[:, 1]], axis=2)
    return jnp.concatenate([top, bot], axis=1)


def kernel(x, c, w_ada, b_ada, norm1, w_in, gmlp_ln_g, gmlp_ln_b, gmlp_w_s, gmlp_b_s, lru_conv_w, lru_conv_b, lru_w_r, lru_b_r, lru_w_i, lru_b_i, lru_lambda, out_norm_gmlp, out_norm_lru, w_out, norm2, w_up, ffn_conv_w, ffn_conv_b, w_down, norm_final):
    depth = w_ada.shape[0]
    assert depth == 1, "the final RMSNorm is fused into the FFN kernel of a single layer"
    bsz, s, d = x.shape
    for l in range(depth):
        mod = _ada_call(c, w_ada[l], b_ada[l][None, :]).reshape(bsz, N_MOD, d)
        bs_full = jnp.repeat(gmlp_b_s[l].T, GMLP_HEAD_DIM, axis=1)
        w_gate = jnp.concatenate(
            [_pair_block_diag(lru_w_r[l]), _pair_block_diag(lru_w_i[l])], axis=2).astype(BF16)
        x1, h2, w_up_bf, w_down_bf = _mixer_call(
            x, mod, norm1[l][None, :], w_in[l].astype(BF16),
            gmlp_ln_g[l][None, :], gmlp_ln_b[l][None, :], gmlp_w_s[l], bs_full,
            lru_conv_w[l], lru_conv_b[l][None, :], w_gate,
            lru_b_r[l][None, :], lru_b_i[l][None, :], lru_lambda[l][None, :],
            out_norm_gmlp[l][None, :], out_norm_lru[l][None, :],
            w_out[l].astype(BF16), norm2[l][None, :], w_up[l], w_down[l])
        x = _ffn_call(h2, x1, mod, w_up_bf, ffn_conv_w[l],
                      ffn_conv_b[l][None, :], w_down_bf, norm_final[None, :])
    return x
```

```python
import functools
import math

import jax
import jax.numpy as jnp
from jax import lax
from jax.experimental import pallas as pl
from jax.experimental.pallas import tpu as pltpu

F32 = jnp.float32
BF16 = jnp.bfloat16

CHUNK = 128
GMLP_HEADS = 8
GMLP_HEAD_DIM = 128
LRU_HEAD_DIM = 64
LRU_C = 8.0
N_MOD = 6
EPS = 1e-6

LANES = 128
SUBLANES = 8
VMEM_LIMIT_BYTES = 56 * 1024 * 1024

ADA_TN = 1024
MIX_TM = 256
MIX_WCOLS = 512
FFN_TM = 512
FFN_TF = 1024

_GELU_C0 = math.sqrt(2.0 / math.pi)
_GELU_C1 = _GELU_C0 * 0.044715


def _gelu(x):
    hx = 0.5 * x
    t = jnp.tanh(x * (_GELU_C0 + _GELU_C1 * (x * x)))
    return hx + hx * t


def _shift_rows(x, prev, k):
    ext = jnp.concatenate([prev, x], axis=0)
    n = x.shape[0]
    return ext[SUBLANES - k:SUBLANES - k + n]


def _scan_steps(a, b, row, shifts):
    n = a.shape[0]
    for k in shifts:
        if k < SUBLANES:
            keep = row >= k
            a_sh = jnp.where(keep, pltpu.roll(a, k, axis=0), 1.0)
            b_sh = jnp.where(keep, pltpu.roll(b, k, axis=0), 0.0)
        else:
            a_sh = jnp.concatenate([jnp.ones((k, a.shape[1]), F32), a[:n - k]], axis=0)
            b_sh = jnp.concatenate([jnp.zeros((k, a.shape[1]), F32), b[:n - k]], axis=0)
        b = b + a * b_sh
        a = a * a_sh
    return a, b


def _linear_recurrence(a, b, h_in, ga_ref, gb_ref):
    n, lanes = a.shape
    ng = n // SUBLANES
    a = a.reshape(ng, SUBLANES, lanes)
    b = b.reshape(ng, SUBLANES, lanes)
    sub = lax.broadcasted_iota(jnp.int32, (1, SUBLANES, lanes), 1)
    for k in (1, 2, 4):
        keep = sub >= k
        a_sh = jnp.where(keep, pltpu.roll(a, k, axis=1), 1.0)
        b_sh = jnp.where(keep, pltpu.roll(b, k, axis=1), 0.0)
        b = b + a * b_sh
        a = a * a_sh
    a = a.reshape(n, lanes)
    b = b.reshape(n, lanes)
    ga_ref[...] = a
    gb_ref[...] = b
    ta = ga_ref[pl.ds(SUBLANES - 1, ng, stride=SUBLANES), :]
    tb = gb_ref[pl.ds(SUBLANES - 1, ng, stride=SUBLANES), :]
    grow = lax.broadcasted_iota(jnp.int32, ta.shape, 0)
    shifts = []
    k = 1
    while k < ng:
        shifts.append(k)
        k *= 2
    ta, tb = _scan_steps(ta, tb, grow, shifts)
    h_end = tb + ta * h_in
    carry_in = jnp.where(grow == 0, h_in, pltpu.roll(h_end, 1, axis=0))
    carry_rows = jnp.concatenate(
        [jnp.broadcast_to(carry_in[g:g + 1, :], (SUBLANES, lanes)) for g in range(ng)], axis=0)
    return b + a * carry_rows


def _ada_kernel(c_ref, w_ref, b_ref, o_ref):
    c = c_ref[...]
    c_act = (c * jax.nn.sigmoid(c)).astype(BF16)
    w = w_ref[...].astype(BF16)
    o_ref[...] = jnp.dot(c_act, w, preferred_element_type=F32) + b_ref[...]


def _ada_call(c, w_ada, b_ada):
    bsz, d = c.shape
    n = w_ada.shape[1]
    return pl.pallas_call(
        _ada_kernel,
        grid=(n // ADA_TN,),
        in_specs=[
            pl.BlockSpec((bsz, d), lambda j: (0, 0)),
            pl.BlockSpec((d, ADA_TN), lambda j: (0, j)),
            pl.BlockSpec((1, ADA_TN), lambda j: (0, j)),
        ],
        out_specs=pl.BlockSpec((bsz, ADA_TN), lambda j: (0, j)),
        out_shape=jax.ShapeDtypeStruct((bsz, n), F32),
        compiler_params=pltpu.CompilerParams(
            dimension_semantics=("arbitrary",), vmem_limit_bytes=VMEM_LIMIT_BYTES),
        name="adaln_mod",
    )(c, w_ada, b_ada)


def _mixer_kernel(*refs, n_in_blocks, n_out_blocks):
    refs = list(refs)
    x_ref, mod_ref, n1_ref = refs[:3]
    win_refs = refs[3:3 + n_in_blocks]
    (lng_ref, lnb_ref, ws_ref, bs_ref, cw_ref, cb_ref, wg_ref, br_ref, bi_ref, lam_ref,
     ong_ref, onl_ref) = refs[3 + n_in_blocks:15 + n_in_blocks]
    wout_refs = refs[15 + n_in_blocks:15 + n_in_blocks + n_out_blocks]
    (n2_ref, wup_ref, wdn_ref, x1_ref, h2_ref, wup_bf_ref, wdn_bf_ref, z_ref, y_ref, hbf_ref,
     mixbf_ref, halo_ref, carry_ref, ga_ref, gb_ref) = refs[15 + n_in_blocks + n_out_blocks:]

    tm, d = x_ref.shape
    dg = GMLP_HEADS * GMLP_HEAD_DIM
    dl = y_ref.shape[1] - dg
    nch = tm // CHUNK

    @pl.when(pl.program_id(1) == 0)
    def _():
        halo_ref[...] = jnp.zeros_like(halo_ref)
        carry_ref[...] = jnp.zeros_like(carry_ref)

    wup_bf_ref[...] = wup_ref[...].astype(BF16)
    wdn_bf_ref[...] = wdn_ref[...].astype(BF16)

    x = x_ref[...]
    inv = lax.rsqrt(jnp.mean(x * x, axis=-1, keepdims=True) + EPS)
    h = x * inv * (n1_ref[...] * (1.0 + mod_ref[1:2, :])) + mod_ref[0:1, :]
    hbf_ref[...] = h.astype(BF16)
    wcols = win_refs[0].shape[1]
    for c, w_ref in enumerate(win_refs):
        z_ref[:, c * wcols:(c + 1) * wcols] = jnp.dot(
            hbf_ref[...], w_ref[...], preferred_element_type=F32)

    r_i = lax.broadcasted_iota(jnp.int32, (CHUNK, CHUNK), 0)
    c_i = lax.broadcasted_iota(jnp.int32, (CHUNK, CHUNK), 1)
    causal = r_i >= c_i
    ssq = jnp.zeros((tm, 1), F32)
    for hd in range(GMLP_HEADS):
        lo = hd * GMLP_HEAD_DIM
        hi = lo + GMLP_HEAD_DIM
        v = _gelu(z_ref[:, dg + lo:dg + hi])
        mu = jnp.mean(v, axis=-1, keepdims=True)
        dv = v - mu
        var = jnp.mean(dv * dv, axis=-1, keepdims=True)
        vn = (dv * lax.rsqrt(var + EPS) * lng_ref[:, lo:hi] + lnb_ref[:, lo:hi]).astype(BF16)
        rhs = jnp.concatenate([vn[n * CHUNK:(n + 1) * CHUNK] for n in range(nch)], axis=1)
        w = jnp.where(causal, ws_ref[hd], 0.0).astype(BF16)
        m = jnp.dot(w, rhs, preferred_element_type=F32)
        bias = bs_ref[:, lo:hi]
        mixed = jnp.concatenate(
            [m[:, n * CHUNK:(n + 1) * CHUNK] + bias for n in range(nch)], axis=0)
        ya = _gelu(z_ref[:, lo:hi]) * mixed
        y_ref[:, lo:hi] = ya
        ssq = ssq + jnp.sum(ya * ya, axis=-1, keepdims=True)
    inv_a = lax.rsqrt(ssq * (1.0 / dg) + EPS)

    ssq = jnp.zeros((tm, 1), F32)
    for g in range(dl // LANES):
        lo = g * LANES
        hi = lo + LANES
        xb = z_ref[:, 2 * dg + lo:2 * dg + hi]
        prev = halo_ref[:, lo:hi]
        halo_ref[:, lo:hi] = xb[tm - SUBLANES:tm]
        kw = cw_ref.shape[0]
        xc = cb_ref[:, lo:hi] + cw_ref[kw - 1:kw, lo:hi] * xb
        for k in range(1, kw):
            xc = xc + cw_ref[kw - 1 - k:kw - k, lo:hi] * _shift_rows(xb, prev, k)
        gates = jnp.dot(xc.astype(BF16), wg_ref[g], preferred_element_type=F32)
        r = jax.nn.sigmoid(gates[:, :LANES] + br_ref[:, lo:hi])
        ig = jax.nn.sigmoid(gates[:, LANES:] + bi_ref[:, lo:hi])
        log_a = (-LRU_C * r) * jax.nn.softplus(-lam_ref[:, lo:hi])
        a = jnp.exp(log_a)
        bterm = jnp.sqrt(jnp.tanh(-log_a) * (1.0 + a * a)) * (ig * xc)
        hseq = _linear_recurrence(a, bterm, carry_ref[SUBLANES - 1:SUBLANES, lo:hi], ga_ref, gb_ref)
        carry_ref[:, lo:hi] = hseq[tm - SUBLANES:tm]
        yb = hseq * _gelu(z_ref[:, 2 * dg + dl + lo:2 * dg + dl + hi])
        y_ref[:, dg + lo:dg + hi] = yb
        ssq = ssq + jnp.sum(yb * yb, axis=-1, keepdims=True)
    inv_b = lax.rsqrt(ssq * (1.0 / dl) + EPS)

    mixbf_ref[:, :dg] = (y_ref[:, :dg] * inv_a * ong_ref[...]).astype(BF16)
    mixbf_ref[:, dg:] = (y_ref[:, dg:] * inv_b * onl_ref[...]).astype(BF16)
    o = jnp.concatenate(
        [jnp.dot(mixbf_ref[...], w_ref[...], preferred_element_type=F32) for w_ref in wout_refs],
        axis=1)
    x1 = x_ref[...] + mod_ref[2:3, :] * o
    x1_ref[...] = x1
    inv2 = lax.rsqrt(jnp.mean(x1 * x1, axis=-1, keepdims=True) + EPS)
    h2 = x1 * inv2 * (n2_ref[...] * (1.0 + mod_ref[4:5, :])) + mod_ref[3:4, :]
    h2_ref[...] = h2.astype(BF16)


def _mixer_call(x, mod, norm1, w_in, ln_g, ln_b, w_s, bs_full, conv_w, conv_b, w_gate,
                b_r, b_i, lam, on_g, on_l, w_out, norm2, w_up, w_down):
    bsz, s, d = x.shape
    tm = MIX_TM
    d_in = w_in.shape[1]
    dmix = w_out.shape[0]
    dl = lam.shape[1]
    tps = s // tm
    n_in_blocks = d_in // MIX_WCOLS
    n_out_blocks = w_out.shape[1] // MIX_WCOLS
    n_steps = bsz * tps

    def row_slab(w):
        return pl.BlockSpec((w.shape[0] // n_steps, w.shape[1]), lambda b, t: (b * tps + t, 0))

    def const(shape):
        zeros = (0,) * len(shape)
        return pl.BlockSpec(shape, lambda b, t: zeros, pipeline_mode=pl.Buffered(1))

    def col_block(w, c):
        return pl.BlockSpec((w.shape[0], MIX_WCOLS), lambda b, t: (0, c),
                            pipeline_mode=pl.Buffered(1))

    tok = pl.BlockSpec((None, tm, d), lambda b, t: (b, t, 0))
    in_specs = (
        [tok, pl.BlockSpec((None, N_MOD, d), lambda b, t: (b, 0, 0)), const(norm1.shape)]
        + [col_block(w_in, c) for c in range(n_in_blocks)]
        + [const(a.shape) for a in (ln_g, ln_b, w_s, bs_full, conv_w, conv_b, w_gate, b_r, b_i,
                                    lam, on_g, on_l)]
        + [col_block(w_out, c) for c in range(n_out_blocks)]
        + [const(norm2.shape), row_slab(w_up), row_slab(w_down)])
    operands = (
        [x, mod, norm1] + [w_in] * n_in_blocks
        + [ln_g, ln_b, w_s, bs_full, conv_w, conv_b, w_gate, b_r, b_i, lam, on_g, on_l]
        + [w_out] * n_out_blocks + [norm2, w_up, w_down])
    return pl.pallas_call(
        functools.partial(_mixer_kernel, n_in_blocks=n_in_blocks, n_out_blocks=n_out_blocks),
        grid=(bsz, tps),
        in_specs=in_specs,
        out_specs=[tok, tok, row_slab(w_up), row_slab(w_down)],
        out_shape=[jax.ShapeDtypeStruct((bsz, s, d), F32),
                   jax.ShapeDtypeStruct((bsz, s, d), BF16),
                   jax.ShapeDtypeStruct(w_up.shape, BF16),
                   jax.ShapeDtypeStruct(w_down.shape, BF16)],
        scratch_shapes=[
            pltpu.VMEM((tm, d_in), F32),
            pltpu.VMEM((tm, dmix), F32),
            pltpu.VMEM((tm, d), BF16),
            pltpu.VMEM((tm, dmix), BF16),
            pltpu.VMEM((SUBLANES, dl), F32),
            pltpu.VMEM((SUBLANES, dl), F32),
            pltpu.VMEM((tm, LANES), F32),
            pltpu.VMEM((tm, LANES), F32),
        ],
        compiler_params=pltpu.CompilerParams(
            dimension_semantics=("arbitrary", "arbitrary"),
            vmem_limit_bytes=VMEM_LIMIT_BYTES),
        name="token_mixer",
    )(*operands)


def _ffn_kernel(h2_ref, x1_hbm, mod_ref, wg_ref, wv_ref, cw_ref, cb_ref, wd_ref, nf_ref,
                o_hbm, acc_ref, halo_ref, x1_buf, o_buf, sems):
    tm = h2_ref.shape[0]
    b = pl.program_id(0)
    t = pl.program_id(1)
    j = pl.program_id(2)
    last_j = pl.num_programs(2) - 1
    tile = b * pl.num_programs(1) + t
    last_tile = pl.num_programs(0) * pl.num_programs(1) - 1
    rows = pl.ds(t * tm, tm)

    def x1_copy():
        return pltpu.make_async_copy(x1_hbm.at[b, rows], x1_buf, sems.at[0])

    def out_copy():
        return pltpu.make_async_copy(o_buf, o_hbm.at[b, rows], sems.at[1])

    @pl.when(j == 0)
    def _():
        x1_copy().start()

        @pl.when(tile > 0)
        def _():
            out_copy().wait()

    h = h2_ref[...]
    kw = cw_ref.shape[0]
    gate = jnp.dot(h, wg_ref[...], preferred_element_type=F32)
    val = jnp.dot(h, wv_ref[...], preferred_element_type=F32)
    prev = jnp.where(t == 0, 0.0, halo_ref[j])
    halo_ref[j] = gate[tm - SUBLANES:tm]
    conv = cb_ref[...] + cw_ref[kw - 1:kw, :] * gate
    for k in range(1, kw):
        conv = conv + cw_ref[kw - 1 - k:kw - k, :] * _shift_rows(gate, prev, k)
    act = (_gelu(conv) * val).astype(BF16)
    contrib = jnp.dot(act, wd_ref[...], preferred_element_type=F32)
    acc_ref[...] = jnp.where(j == 0, contrib, acc_ref[...] + contrib)

    @pl.when(j == last_j)
    def _():
        x1_copy().wait()
        xo = x1_buf[...] + mod_ref[N_MOD - 1:N_MOD, :] * acc_ref[...]
        inv = lax.rsqrt(jnp.mean(xo * xo, axis=-1, keepdims=True) + EPS)
        o_buf[...] = xo * inv * nf_ref[...]
        out_copy().start()

        @pl.when(tile == last_tile)
        def _():
            out_copy().wait()


def _ffn_call(h2, x1, mod, w_up, conv_w, conv_b, w_down, norm_final):
    bsz, s, d = x1.shape
    f = w_down.shape[0]
    tm, tf = FFN_TM, FFN_TF
    nf = f // tf
    tok = pl.BlockSpec((None, tm, d), lambda b, t, j: (b, t, 0))
    return pl.pallas_call(
        _ffn_kernel,
        grid=(bsz, s // tm, nf),
        in_specs=[
            tok,
            pl.BlockSpec(memory_space=pl.ANY),
            pl.BlockSpec((None, N_MOD, d), lambda b, t, j: (b, 0, 0)),
            pl.BlockSpec((d, tf), lambda b, t, j: (0, j)),
            pl.BlockSpec((d, tf), lambda b, t, j: (0, j + nf)),
            pl.BlockSpec((conv_w.shape[0], tf), lambda b, t, j: (0, j)),
            pl.BlockSpec((1, tf), lambda b, t, j: (0, j)),
            pl.BlockSpec((tf, d), lambda b, t, j: (j, 0)),
            pl.BlockSpec((1, d), lambda b, t, j: (0, 0)),
        ],
        out_specs=pl.BlockSpec(memory_space=pl.ANY),
        out_shape=jax.ShapeDtypeStruct((bsz, s, d), F32),
        scratch_shapes=[
            pltpu.VMEM((tm, d), F32),
            pltpu.VMEM((nf, SUBLANES, tf), F32),
            pltpu.VMEM((tm, d), F32),
            pltpu.VMEM((tm, d), F32),
            pltpu.SemaphoreType.DMA((2,)),
        ],
        compiler_params=pltpu.CompilerParams(
            dimension_semantics=("arbitrary", "arbitrary", "arbitrary"),
            vmem_limit_bytes=VMEM_LIMIT_BYTES),
        name="conv_ffn",
    )(h2, x1, mod, w_up, w_up, conv_w, conv_b, w_down, norm_final)


def _pair_block_diag(w):
    hh, hd, _ = w.shape
    w = w.reshape(hh // 2, 2, hd, hd)
    z = jnp.zeros((hh // 2, hd, hd), w.dtype)
    top = jnp.concatenate([w[:, 0], z], axis=2)
    bot = jnp.concatenate([z, w[:, 1]], axis=2)
    return jnp.concatenate([top, bot], axis=1)


def kernel(x, c, w_ada, b_ada, norm1, w_in, gmlp_ln_g, gmlp_ln_b, gmlp_w_s, gmlp_b_s, lru_conv_w, lru_conv_b, lru_w_r, lru_b_r, lru_w_i, lru_b_i, lru_lambda, out_norm_gmlp, out_norm_lru, w_out, norm2, w_up, ffn_conv_w, ffn_conv_b, w_down, norm_final):
    depth = w_ada.shape[0]
    assert depth == 1, "the final RMSNorm is fused into the FFN kernel of a single layer"
    bsz, s, d = x.shape
    for l in range(depth):
        mod = _ada_call(c, w_ada[l], b_ada[l][None, :]).reshape(bsz, N_MOD, d)
        bs_full = jnp.repeat(gmlp_b_s[l].T, GMLP_HEAD_DIM, axis=1)
        w_gate = jnp.concatenate(
            [_pair_block_diag(lru_w_r[l]), _pair_block_diag(lru_w_i[l])], axis=2).astype(BF16)
        x1, h2, w_up_bf, w_down_bf = _mixer_call(
            x, mod, norm1[l][None, :], w_in[l].astype(BF16),
            gmlp_ln_g[l][None, :], gmlp_ln_b[l][None, :], gmlp_w_s[l], bs_full,
            lru_conv_w[l], lru_conv_b[l][None, :], w_gate,
            lru_b_r[l][None, :], lru_b_i[l][None, :], lru_lambda[l][None, :],
            out_norm_gmlp[l][None, :], out_norm_lru[l][None, :],
            w_out[l].astype(BF16), norm2[l][None, :], w_up[l], w_down[l])
        x = _ffn_call(h2, x1, mod, w_up_bf, ffn_conv_w[l],
                      ffn_conv_b[l][None, :], w_down_bf, norm_final[None, :])
    return x
```
